```python
import math
import jax, jax.numpy as jnp
from jax import lax
import numpy as np

D_MODEL = 4096
BATCH = 2
SEQ = 8192
DEPTH = 2
DEC_BATCH = 1
DEC_SEQ = 8192
PAST_LEN = 128

ATTN_HEADS = 8
ATTN_HEAD_DIM = 128
ATTN_WIDTH = ATTN_HEADS * 2 * ATTN_HEAD_DIM
Q_BLOCK = 128
DN_HEADS = 16
DN_HEAD_DIM = 128
DN_WIDTH = DN_HEADS * DN_HEAD_DIM
CONV_K = 5
CHUNK = 64
PEER_HEADS = 8
N_KEYS = 128
N_EXPERTS = N_KEYS * N_KEYS
PEER_QDIM = 256
PEER_TOPK = 16
TOKEN_BLOCK = 128
ALPHA = (2.0 * DEPTH) ** 0.25
BETA = (8.0 * DEPTH) ** -0.25
LN_EPS = 1e-5
RMS_EPS = 1e-6
IN_SIZES = (ATTN_WIDTH, ATTN_WIDTH, ATTN_WIDTH,
            3 * DN_WIDTH, DN_WIDTH,
            2 * DN_HEADS, 2 * DN_HEADS,
            D_MODEL, D_MODEL)
IN_COLS = 3 * ATTN_WIDTH + 4 * DN_WIDTH + 4 * DN_HEADS + 2 * D_MODEL

kernel_name = "hybrid_diffattn_gdn_peer_encoder"


def _split_points(sizes):
    pts, acc = [], 0
    for s in sizes[:-1]:
        acc += s
        pts.append(acc)
    return pts


def layer_norm(x, g=None, b=None):
    xf = x.astype(jnp.float32)
    mu = jnp.mean(xf, axis=-1, keepdims=True)
    var = jnp.mean(jnp.square(xf - mu), axis=-1, keepdims=True)
    y = (xf - mu) * lax.rsqrt(var + LN_EPS)
    if g is not None:
        y = y * g.astype(jnp.float32) + b.astype(jnp.float32)
    return y.astype(x.dtype)


def rms_norm(x, w):
    xf = x.astype(jnp.float32)
    y = xf * lax.rsqrt(jnp.mean(jnp.square(xf), axis=-1, keepdims=True) + RMS_EPS)
    return y * w.astype(jnp.float32)


def l2_normalize(x):
    return x * lax.rsqrt(jnp.sum(jnp.square(x), axis=-1, keepdims=True) + RMS_EPS)


def centred_depthwise_conv(x, w):
    pad = (CONV_K - 1) // 2
    return lax.conv_general_dilated(x, w.astype(x.dtype)[:, None, :], window_strides=(1,),
                                    padding=[(pad, pad)], dimension_numbers=('NWC', 'WIO', 'NWC'),
                                    feature_group_count=x.shape[-1])


def diff_attention(q, k, v, lam):
    B, S = q.shape[0], q.shape[1]
    scale = ATTN_HEAD_DIM ** -0.5
    slopes = jnp.exp2(-8.0 * jnp.arange(1, ATTN_HEADS + 1, dtype=jnp.float32) / ATTN_HEADS)
    kpos = jnp.arange(S, dtype=jnp.float32)
    nb = S // Q_BLOCK
    qb = q.reshape(B, nb, Q_BLOCK, ATTN_HEADS, 2, ATTN_HEAD_DIM).transpose(1, 0, 2, 3, 4, 5)

    def block(args):
        qi, i = args
        s = jnp.einsum('bqhcd,bkhcd->bhcqk', qi, k, preferred_element_type=jnp.float32) * scale
        qpos = (i * Q_BLOCK + jnp.arange(Q_BLOCK)).astype(jnp.float32)
        dist = jnp.abs(qpos[:, None] - kpos[None, :])
        s = s - slopes[:, None, None, None] * dist
        p = jax.nn.softmax(s, axis=-1)
        a = p[:, :, 0] - lam * p[:, :, 1]
        return jnp.einsum('bhqk,bkhe->bqhe', a.astype(v.dtype), v)

    o = lax.map(block, (qb, jnp.arange(nb)))
    return o.transpose(1, 0, 2, 3, 4).reshape(B, S, ATTN_HEADS, 2 * ATTN_HEAD_DIM)


def gated_delta_chunked(q, k, v, g, beta):
    B, H, S, dk = q.shape
    dv = v.shape[-1]
    n = S // CHUNK
    q = q.reshape(B, H, n, CHUNK, dk)
    k = k.reshape(B, H, n, CHUNK, dk)
    v = v.reshape(B, H, n, CHUNK, dv)
    beta = beta.reshape(B, H, n, CHUNK)
    g = jnp.cumsum(g.reshape(B, H, n, CHUNK), axis=-1)
    kb = k * beta[..., None]
    vb = v * beta[..., None]
    tril = jnp.tril(jnp.ones((CHUNK, CHUNK), dtype=bool))
    strict = jnp.tril(jnp.ones((CHUNK, CHUNK), dtype=bool), -1)
    diff = g[..., :, None] - g[..., None, :]
    decay = jnp.where(tril, jnp.exp(jnp.where(tril, diff, 0.0)), 0.0)
    m = jnp.where(strict, jnp.einsum('bhncd,bhnsd->bhncs', kb, k) * decay, 0.0)
    a_mat = m + jnp.eye(CHUNK, dtype=jnp.float32)
    rhs = jnp.concatenate([vb, kb * jnp.exp(g)[..., None]], axis=-1)
    sol = lax.linalg.triangular_solve(a_mat, rhs, left_side=True, lower=True, unit_diagonal=True)
    u, w = sol[..., :dv], sol[..., dv:]
    qk = jnp.einsum('bhncd,bhnsd->bhncs', q, k) * decay

    def step(state, inp):
        qc, kc, uc, wc, qkc, gc = inp
        v_new = uc - jnp.einsum('bhcd,bhde->bhce', wc, state)
        o = (jnp.einsum('bhcd,bhde->bhce', qc * jnp.exp(gc)[..., None], state)
             + jnp.einsum('bhcs,bhse->bhce', qkc, v_new))
        glast = gc[..., -1]
        state = (state * jnp.exp(glast)[..., None, None]
                 + jnp.einsum('bhcd,bhce->bhde', kc * jnp.exp(glast[..., None] - gc)[..., None], v_new))
        return state, o

    xs = tuple(jnp.moveaxis(t, 2, 0) for t in (q, k, u, w, qk, g))
    state0 = jnp.zeros((B, H, dk, dv), jnp.float32)
    _, o = lax.scan(step, state0, xs)
    return jnp.moveaxis(o, 0, 2).reshape(B, H, S, dv)


def token_mixer(h, layer_idx, w_in, lambda_qk, attn_subln_w, dn_conv_w, dn_a_log, dn_dt_bias,
                dn_norm_w, w_br_attn, w_br_dn, w_out):
    B, S, _ = h.shape
    proj = h @ w_in
    aq, ak, av, dqkv, dz, da, db, ga, gd = jnp.split(proj, _split_points(IN_SIZES), axis=-1)

    aq = aq.reshape(B, S, ATTN_HEADS, 2, ATTN_HEAD_DIM)
    ak = ak.reshape(B, S, ATTN_HEADS, 2, ATTN_HEAD_DIM)
    av = av.reshape(B, S, ATTN_HEADS, 2 * ATTN_HEAD_DIM)
    lam_init = 0.8 - 0.6 * math.exp(-0.3 * layer_idx)
    lq = lambda_qk.astype(jnp.float32)
    lam = jnp.exp(jnp.sum(lq[0] * lq[1])) - jnp.exp(jnp.sum(lq[2] * lq[3])) + lam_init
    oa = diff_attention(aq, ak, av, lam)
    oa = (rms_norm(oa, attn_subln_w) * (1.0 - lam_init)).astype(h.dtype).reshape(B, S, ATTN_WIDTH)

    dqkv = jax.nn.silu(centred_depthwise_conv(dqkv, dn_conv_w))
    dq, dk, dv = jnp.split(dqkv, 3, axis=-1)
    to_heads = lambda t: t.reshape(B, S, DN_HEADS, DN_HEAD_DIM).transpose(0, 2, 1, 3).astype(jnp.float32)
    dq = l2_normalize(to_heads(dq)) * (DN_HEAD_DIM ** -0.5)
    dk = l2_normalize(to_heads(dk))
    dv = to_heads(dv)
    da = da.reshape(B, S, 2, DN_HEADS).astype(jnp.float32)
    db = db.reshape(B, S, 2, DN_HEADS).astype(jnp.float32)
    g = -jnp.exp(dn_a_log.astype(jnp.float32)) * jax.nn.softplus(da + dn_dt_bias.astype(jnp.float32))
    beta = jax.nn.sigmoid(db)
    g = g.transpose(2, 0, 3, 1)
    beta = beta.transpose(2, 0, 3, 1)
    o_fwd = gated_delta_chunked(dq, dk, dv, g[0], beta[0])
    o_bwd = jnp.flip(gated_delta_chunked(jnp.flip(dq, 2), jnp.flip(dk, 2), jnp.flip(dv, 2),
                                         jnp.flip(g[1], -1), jnp.flip(beta[1], -1)), 2)
    od = (o_fwd + o_bwd).transpose(0, 2, 1, 3)
    od = rms_norm(od, dn_norm_w) * jax.nn.silu(dz.reshape(B, S, DN_HEADS, DN_HEAD_DIM).astype(jnp.float32))
    od = od.astype(h.dtype).reshape(B, S, DN_WIDTH)

    merged = jax.nn.sigmoid(ga) * (oa @ w_br_attn) + jax.nn.sigmoid(gd) * (od @ w_br_dn)
    return merged @ w_out


def peer_ffn(h, peer_wq, peer_subkeys, peer_u, peer_v):
    B, S, D = h.shape
    T = B * S
    x = h.reshape(T, D)
    q = (x @ peer_wq).reshape(T, PEER_HEADS, 2, PEER_QDIM // 2)
    s = jnp.einsum('thpd,hpnd->thpn', q, peer_subkeys, preferred_element_type=jnp.float32)
    sv, si = lax.top_k(s, PEER_TOPK)
    cand = (sv[:, :, 0, :, None] + sv[:, :, 1, None, :]).reshape(T, PEER_HEADS, PEER_TOPK * PEER_TOPK)
    fv, fi = lax.top_k(cand, PEER_TOPK)
    i1 = jnp.take_along_axis(si[:, :, 0], fi // PEER_TOPK, axis=-1)
    i2 = jnp.take_along_axis(si[:, :, 1], fi % PEER_TOPK, axis=-1)
    idx = (i1 * N_KEYS + i2).reshape(T, PEER_HEADS * PEER_TOPK)
    gate = jax.nn.softmax(fv, axis=-1).reshape(T, PEER_HEADS * PEER_TOPK)
    nb = T // TOKEN_BLOCK

    def block(args):
        xb, ib, gb = args
        act = jax.nn.gelu(jnp.einsum('ted,td->te', peer_u[ib], xb, preferred_element_type=jnp.float32),
                          approximate=False)
        wgt = (gb * act).astype(xb.dtype)
        return jnp.einsum('te,ted->td', wgt, peer_v[ib])

    out = lax.map(block, (x.reshape(nb, TOKEN_BLOCK, D), idx.reshape(nb, TOKEN_BLOCK, -1),
                          gate.reshape(nb, TOKEN_BLOCK, -1)))
    return out.reshape(B, S, D)


def encoder_layer(x, c, layer_idx, w_ada, b_ada, w_in, lambda_qk, attn_subln_w, dn_conv_w, dn_a_log,
                  dn_dt_bias, dn_norm_w, w_br_attn, w_br_dn, w_out, ln1_g, ln1_b, peer_wq, peer_subkeys,
                  peer_u, peer_v, ln2_g, ln2_b):
    mod = jax.nn.silu(c) @ w_ada + b_ada
    sh1, sc1, g1, sh2, sc2, g2 = jnp.split(mod[:, None, :], 6, axis=-1)
    h = layer_norm(x) * (1 + sc1) + sh1
    y = token_mixer(h, layer_idx, w_in, lambda_qk, attn_subln_w, dn_conv_w, dn_a_log, dn_dt_bias,
                    dn_norm_w, w_br_attn, w_br_dn, w_out)
    x = layer_norm(ALPHA * x + g1 * y, ln1_g, ln1_b)
    h = layer_norm(x) * (1 + sc2) + sh2
    y = peer_ffn(h, peer_wq, peer_subkeys, peer_u, peer_v)
    x = layer_norm(ALPHA * x + g2 * y, ln2_g, ln2_b)
    return x


def run_trunk(x, c, weights):
    for l in range(DEPTH):
        x = encoder_layer(x, c, l, *[w[l] for w in weights])
    return x


def setup_inputs(seed: int = 0) -> dict:
    key = jax.random.key(seed)
    ks = jax.random.split(key, 32)
    f32 = jnp.float32
    nrm = lambda k, shape, s: jax.random.normal(k, shape, f32) * s
    L, D = DEPTH, D_MODEL
    dt = jnp.exp(jax.random.uniform(ks[10], (L, 2, DN_HEADS), f32, math.log(1e-3), math.log(1e-1)))
    return {
        "x_prompt": nrm(ks[0], (BATCH, SEQ, D), 1.0),
        "x_sample": nrm(ks[1], (DEC_BATCH, DEC_SEQ, D), 1.0),
        "c_prompt": nrm(ks[2], (BATCH, D), 1.0),
        "c_sample": nrm(ks[3], (DEC_BATCH, D), 1.0),
        "w_ada": nrm(ks[4], (L, D, 6 * D), D ** -0.5),
        "b_ada": nrm(ks[5], (L, 6 * D), 0.02),
        "w_in": nrm(ks[6], (L, D, IN_COLS), D ** -0.5),
        "lambda_qk": nrm(ks[7], (L, 4, ATTN_HEAD_DIM), 0.1),
        "attn_subln_w": 1.0 + nrm(ks[8], (L, 2 * ATTN_HEAD_DIM), 0.02),
        "dn_conv_w": nrm(ks[9], (L, CONV_K, 3 * DN_WIDTH), CONV_K ** -0.5),
        "dn_a_log": jnp.log(jax.random.uniform(ks[11], (L, 2, DN_HEADS), f32, 1.0, 16.0)),
        "dn_dt_bias": dt + jnp.log(-jnp.expm1(-dt)),
        "dn_norm_w": 1.0 + nrm(ks[12], (L, DN_HEAD_DIM), 0.02),
        "w_br_attn": nrm(ks[13], (L, ATTN_WIDTH, D), BETA * ATTN_WIDTH ** -0.5),
        "w_br_dn": nrm(ks[14], (L, DN_WIDTH, D), BETA * DN_WIDTH ** -0.5),
        "w_out": nrm(ks[15], (L, D, D), BETA * D ** -0.5),
        "ln1_g": 1.0 + nrm(ks[16], (L, D), 0.02),
        "ln1_b": nrm(ks[17], (L, D), 0.02),
        "peer_wq": nrm(ks[18], (L, D, PEER_HEADS * PEER_QDIM), D ** -0.5),
        "peer_subkeys": nrm(ks[19], (L, PEER_HEADS, 2, N_KEYS, PEER_QDIM // 2), (PEER_QDIM // 2) ** -0.5),
        "peer_u": nrm(ks[20], (L, N_EXPERTS, D), D ** -0.5),
        "peer_v": nrm(ks[21], (L, N_EXPERTS, D), BETA * PEER_HEADS ** -0.5),
        "ln2_g": 1.0 + nrm(ks[22], (L, D), 0.02),
        "ln2_b": nrm(ks[23], (L, D), 0.02),
    }


def reference(x_prompt, x_sample, c_prompt, c_sample, w_ada, b_ada, w_in, lambda_qk, attn_subln_w,
              dn_conv_w, dn_a_log, dn_dt_bias, dn_norm_w, w_br_attn, w_br_dn, w_out, ln1_g, ln1_b,
              peer_wq, peer_subkeys, peer_u, peer_v, ln2_g, ln2_b):
    weights = (w_ada, b_ada, w_in, lambda_qk, attn_subln_w, dn_conv_w, dn_a_log, dn_dt_bias, dn_norm_w,
               w_br_attn, w_br_dn, w_out, ln1_g, ln1_b, peer_wq, peer_subkeys, peer_u, peer_v, ln2_g, ln2_b)
    y_prompt = run_trunk(x_prompt, c_prompt, weights)
    y_sample = run_trunk(x_sample, c_sample, weights)
    return (y_prompt, y_sample)
```

```python
import functools
import math

import jax
import jax.numpy as jnp
from jax import lax
from jax.experimental import pallas as pl
from jax.experimental.pallas import tpu as pltpu

F32 = jnp.float32
BF16 = jnp.bfloat16

V7X_VMEM_BYTES = 64 * 1024 * 1024
VMEM_LIMIT = V7X_VMEM_BYTES - 8 * 1024 * 1024
LANES = 128

ATTN_HEADS = 8
ATTN_HEAD_DIM = 128
DN_HEADS = 16
DN_HEAD_DIM = 128
CONV_K = 5
CHUNK = 64
SUPER = 256
PEER_HEADS = 8
N_KEYS = 128
PEER_TOPK = 16
LN_EPS = 1e-5
RMS_EPS = 1e-6
LOG2E = 1.4426950408889634
NEG_BIG = -1e30


def _cparams(*sem):
    return pltpu.CompilerParams(dimension_semantics=sem, vmem_limit_bytes=VMEM_LIMIT)


def _tile(n, pref):
    t = min(n, pref)
    assert n % t == 0, (n, pref)
    return t


def _dot(a, b):
    return jnp.dot(a, b, preferred_element_type=F32)


def _dot_nt(a, b):
    return lax.dot_general(a, b, (((1,), (1,)), ((), ())), preferred_element_type=F32)


def _dot_tn(a, b):
    return lax.dot_general(a, b, (((0,), (0,)), ((), ())), preferred_element_type=F32)


def _sigmoid(x):
    return 1.0 / (1.0 + jnp.exp(-x))


def _layer_norm(x):
    mu = jnp.mean(x, axis=-1, keepdims=True)
    xc = x - mu
    var = jnp.mean(xc * xc, axis=-1, keepdims=True)
    return xc * lax.rsqrt(var + LN_EPS)


def _mm_kernel(a_ref, b_ref, o_ref):
    o_ref[...] = _dot(a_ref[...], b_ref[...]).astype(o_ref.dtype)


def _matmul(a, b, out_dtype, name, tm=1024, tn=1024):
    m, k = a.shape
    n = b.shape[1]
    tm, tn = _tile(m, tm), _tile(n, tn)
    return pl.pallas_call(
        _mm_kernel,
        grid=(m // tm, n // tn),
        in_specs=[pl.BlockSpec((tm, k), lambda i, j: (i, 0)),
                  pl.BlockSpec((k, tn), lambda i, j: (0, j))],
        out_specs=pl.BlockSpec((tm, tn), lambda i, j: (i, j)),
        out_shape=jax.ShapeDtypeStruct((m, n), out_dtype),
        compiler_params=_cparams("parallel", "arbitrary"),
        name=name,
    )(a, b)


def _ada_kernel(c_ref, w_ref, b_ref, o_ref):
    c = c_ref[...]
    s = (c * _sigmoid(c)).astype(BF16)
    o_ref[...] = _dot(s, w_ref[...].astype(BF16)) + b_ref[...]


def _ada(c8, w, b):
    d, n = w.shape
    tn = _tile(n, 512)
    return pl.pallas_call(
        _ada_kernel,
        grid=(n // tn,),
        in_specs=[pl.BlockSpec((8, d), lambda j: (0, 0)),
                  pl.BlockSpec((d, tn), lambda j: (0, j)),
                  pl.BlockSpec((1, tn), lambda j: (0, j))],
        out_specs=pl.BlockSpec((8, tn), lambda j: (0, j)),
        out_shape=jax.ShapeDtypeStruct((8, n), F32),
        compiler_params=_cparams("arbitrary"),
        name="ada_mod",
    )(c8, w, b.reshape(1, n))


def _lnmod_kernel(x_ref, sc_ref, sh_ref, h_ref):
    h_ref[0] = (_layer_norm(x_ref[0]) * (1.0 + sc_ref[0]) + sh_ref[0]).astype(h_ref.dtype)


def _lnmod(x, sc, sh):
    bsz, s, d = x.shape
    tr = _tile(s, 256)
    row = pl.BlockSpec((1, tr, d), lambda b, i: (b, i, 0))
    vec = pl.BlockSpec((1, 1, d), lambda b, i: (b, 0, 0))
    return pl.pallas_call(
        _lnmod_kernel,
        grid=(bsz, s // tr),
        in_specs=[row, vec, vec],
        out_specs=row,
        out_shape=jax.ShapeDtypeStruct((bsz, s, d), BF16),
        compiler_params=_cparams("parallel", "parallel"),
        name="ln_mod",
    )(x, sc, sh)


def _res_ln_kernel(x_ref, y_ref, g_ref, lg_ref, lb_ref, *rest, alpha, with_h):
    z = alpha * x_ref[0] + g_ref[0] * y_ref[0].astype(F32)
    x1 = _layer_norm(z) * lg_ref[...] + lb_ref[...]
    if with_h:
        sc_ref, sh_ref, xo_ref, h_ref = rest
        h_ref[0] = (_layer_norm(x1) * (1.0 + sc_ref[0]) + sh_ref[0]).astype(h_ref.dtype)
    else:
        (xo_ref,) = rest
    xo_ref[0] = x1


def _res_ln(x, y, gate, ln_g, ln_b, alpha, sc=None, sh=None):
    bsz, s, d = x.shape
    tr = _tile(s, 256)
    row = pl.BlockSpec((1, tr, d), lambda b, i: (b, i, 0))
    vec = pl.BlockSpec((1, 1, d), lambda b, i: (b, 0, 0))
    par = pl.BlockSpec((1, d), lambda b, i: (0, 0))
    with_h = sc is not None
    ins = [x, y, gate, ln_g.reshape(1, d), ln_b.reshape(1, d)]
    in_specs = [row, row, vec, par, par]
    out_shape = [jax.ShapeDtypeStruct((bsz, s, d), F32)]
    out_specs = [row]
    if with_h:
        ins += [sc, sh]
        in_specs += [vec, vec]
        out_shape.append(jax.ShapeDtypeStruct((bsz, s, d), BF16))
        out_specs.append(row)
    out = pl.pallas_call(
        functools.partial(_res_ln_kernel, alpha=alpha, with_h=with_h),
        grid=(bsz, s // tr),
        in_specs=in_specs,
        out_specs=out_specs,
        out_shape=out_shape,
        compiler_params=_cparams("parallel", "parallel"),
        name="res_ln",
    )(*ins)
    return (out[0], out[1]) if with_h else (out[0], None)


def _attn_kernel(lq_ref, w_ref, q_ref, k_ref, v_ref, o_ref, m_scr, l_scr, acc_scr, *, tq, tk, lam_init):
    dh = ATTN_HEAD_DIM
    h = pl.program_id(1)
    i = pl.program_id(2)
    nk = k_ref.shape[1] // tk
    slope2 = jnp.exp2(-(jnp.full((1, 1), h + 1, jnp.int32).astype(F32))) * LOG2E
    c1 = dh ** -0.5 * LOG2E
    qpos = (i * tq + lax.broadcasted_iota(jnp.int32, (tq, 1), 0)).astype(F32)
    q = q_ref[0]

    m_scr[...] = jnp.full(m_scr.shape, NEG_BIG, F32)
    l_scr[...] = jnp.zeros(l_scr.shape, F32)
    acc_scr[...] = jnp.zeros(acc_scr.shape, F32)

    def body(j, carry):
        start = pl.multiple_of(j * tk, tk)
        k = k_ref[0, pl.ds(start, tk), :]
        v = v_ref[0, pl.ds(start, tk), :]
        kpos = (j * tk + lax.broadcasted_iota(jnp.int32, (1, tk), 1)).astype(F32)
        bias = slope2 * jnp.abs(qpos - kpos)
        for c in range(2):
            s = _dot_nt(q[:, c * dh:(c + 1) * dh], k[:, c * dh:(c + 1) * dh]) * c1 - bias
            m_old = m_scr[c]
            m_new = jnp.maximum(m_old, jnp.max(s, axis=1, keepdims=True))
            alpha = jnp.exp2(m_old - m_new)
            p = jnp.exp2(s - m_new[:, :1])
            l_scr[c] = alpha * l_scr[c] + jnp.sum(p, axis=1, keepdims=True)
            acc_scr[c] = alpha[:, :1] * acc_scr[c] + _dot(p.astype(BF16), v)
            m_scr[c] = m_new
        return carry

    lax.fori_loop(0, nk, body, 0)

    lq = lq_ref[...]
    lam = (jnp.exp(jnp.sum(lq[0:1] * lq[1:2], axis=1, keepdims=True))
           - jnp.exp(jnp.sum(lq[2:3] * lq[3:4], axis=1, keepdims=True)) + lam_init)
    o = acc_scr[0] / l_scr[0][:, :1] - lam * (acc_scr[1] / l_scr[1][:, :1])
    ms = jnp.mean(o * o, axis=1, keepdims=True)
    o_ref[0] = (o * lax.rsqrt(ms + RMS_EPS) * w_ref[...] * (1.0 - lam_init)).astype(o_ref.dtype)


def _diff_attention(qkv, lambda_qk, subln_w, lam_init):
    bsz, s, _ = qkv.shape
    hw = 2 * ATTN_HEAD_DIM
    tq = _tile(s, 512)
    tk = _tile(s, 512)
    return pl.pallas_call(
        functools.partial(_attn_kernel, tq=tq, tk=tk, lam_init=lam_init),
        grid=(bsz, ATTN_HEADS, s // tq),
        in_specs=[pl.BlockSpec((4, ATTN_HEAD_DIM), lambda b, h, i: (0, 0)),
                  pl.BlockSpec((1, hw), lambda b, h, i: (0, 0)),
                  pl.BlockSpec((1, tq, hw), lambda b, h, i: (b, i, h)),
                  pl.BlockSpec((1, s, hw), lambda b, h, i: (b, 0, ATTN_HEADS + h)),
                  pl.BlockSpec((1, s, hw), lambda b, h, i: (b, 0, 2 * ATTN_HEADS + h))],
        out_specs=pl.BlockSpec((1, tq, hw), lambda b, h, i: (b, i, h)),
        out_shape=jax.ShapeDtypeStruct((bsz, s, ATTN_HEADS * hw), BF16),
        scratch_shapes=[pltpu.VMEM((2, tq, LANES), F32),
                        pltpu.VMEM((2, tq, LANES), F32),
                        pltpu.VMEM((2, tq, hw), F32)],
        compiler_params=_cparams("parallel", "parallel", "arbitrary"),
        name="diff_attn",
    )(lambda_qk, subln_w.reshape(1, hw), qkv, qkv, qkv)


def _conv_kernel(prev_ref, cur_ref, next_ref, w_ref, o_ref, *, ts, halo):
    i = pl.program_id(1)
    p = pl.program_id(2)
    ns = pl.num_programs(1)
    dh = DN_HEAD_DIM
    x = cur_ref[0].astype(F32)
    pv = jnp.where(i > 0, prev_ref[0].astype(F32), 0.0)
    nx = jnp.where(i < ns - 1, next_ref[0].astype(F32), 0.0)
    w = w_ref[...]
    row = lax.broadcasted_iota(jnp.int32, (ts, 1), 0)
    xm1 = jnp.where(row == 0, pv[halo - 1:halo], pltpu.roll(x, 1, 0))
    xm2 = jnp.where(row == 0, pv[halo - 2:halo - 1], jnp.where(row == 1, pv[halo - 1:halo], pltpu.roll(x, 2, 0)))
    xp1 = jnp.where(row == ts - 1, nx[0:1], pltpu.roll(x, ts - 1, 0))
    xp2 = jnp.where(row == ts - 2, nx[0:1], jnp.where(row == ts - 1, nx[1:2], pltpu.roll(x, ts - 2, 0)))
    y = w[0:1] * xm2 + w[1:2] * xm1 + w[2:3] * x + w[3:4] * xp1 + w[4:5] * xp2
    y = y * _sigmoid(y)
    qscale = jnp.where(p == 0, dh ** -0.5, 1.0)
    for hh in range(y.shape[1] // dh):
        seg = y[:, hh * dh:(hh + 1) * dh]
        nrm = lax.rsqrt(jnp.sum(seg * seg, axis=1, keepdims=True) + RMS_EPS) * qscale
        fac = jnp.where(p == 2, 1.0, nrm)
        o_ref[0, :, hh * dh:(hh + 1) * dh] = (seg * fac).astype(o_ref.dtype)


def _dn_conv(dn, conv_w):
    bsz, s, _ = dn.shape
    cw = DN_HEADS * DN_HEAD_DIM
    ts = _tile(s, 256)
    halo = 16
    hb = ts // halo
    nhb = s // halo
    return pl.pallas_call(
        functools.partial(_conv_kernel, ts=ts, halo=halo),
        grid=(bsz, s // ts, 3),
        in_specs=[pl.BlockSpec((1, halo, cw), lambda b, i, p: (b, jnp.maximum(i * hb - 1, 0), p)),
                  pl.BlockSpec((1, ts, cw), lambda b, i, p: (b, i, p)),
                  pl.BlockSpec((1, halo, cw), lambda b, i, p: (b, jnp.minimum((i + 1) * hb, nhb - 1), p)),
                  pl.BlockSpec((CONV_K, cw), lambda b, i, p: (0, p))],
        out_specs=pl.BlockSpec((1, ts, cw), lambda b, i, p: (b, i, p)),
        out_shape=jax.ShapeDtypeStruct((bsz, s, 3 * cw), BF16),
        compiler_params=_cparams("parallel", "parallel", "parallel"),
        name="dn_conv",
    )(dn, dn, dn, conv_w)


def _split3(x):
    hi = x.astype(BF16)
    r1 = x - hi.astype(F32)
    mid = r1.astype(BF16)
    lo = (r1 - mid.astype(F32)).astype(BF16)
    return hi, mid, lo


def _gdn_kernel(q_ref, k_ref, v_ref, dac_ref, dbc_ref, dar_ref, al_ref, dtb_ref, o_ref, s_scr, *, ts, rev):
    dh = DN_HEAD_DIM
    h = pl.program_id(1)
    i = pl.program_id(2)
    nsc = ts // SUPER

    @pl.when(i == 0)
    def _():
        s_scr[...] = jnp.zeros(s_scr.shape, F32)

    r = lax.broadcasted_iota(jnp.int32, (SUPER, SUPER), 0)
    c = lax.broadcasted_iota(jnp.int32, (SUPER, SUPER), 1)
    same = (r // CHUNK) == (c // CHUNK)
    if rev:
        incl = jnp.logical_and(same, c >= r)
        strict = jnp.logical_and(same, c > r)
    else:
        incl = jnp.logical_and(same, c <= r)
        strict = jnp.logical_and(same, c < r)
    incl01 = jnp.where(incl, 1.0, 0.0).astype(BF16)
    same01 = jnp.where(same, 1.0, 0.0).astype(BF16)
    eye = jnp.where(r == c, 1.0, 0.0)
    lvl = 31 - lax.clz(lax.bitwise_xor(r, c))

    neg_a = -jnp.exp(al_ref[0])
    dtb = dtb_ref[0]
    hsel = lax.broadcasted_iota(jnp.int32, (1, DN_HEADS), 1) == h

    def softplus(x):
        return jnp.maximum(x, 0.0) + jnp.log1p(jnp.exp(-jnp.abs(x)))

    for sc in (range(nsc - 1, -1, -1) if rev else range(nsc)):
        rows = slice(sc * SUPER, (sc + 1) * SUPER)
        q = q_ref[0, rows, :]
        k = k_ref[0, rows, :]
        v = v_ref[0, rows, :]
        kf = k.astype(F32)
        da_col = jnp.sum(jnp.where(hsel, dac_ref[0, rows, :], 0.0), axis=1, keepdims=True)
        db_col = jnp.sum(jnp.where(hsel, dbc_ref[0, rows, :], 0.0), axis=1, keepdims=True)
        g_col = jnp.broadcast_to(neg_a * softplus(da_col + dtb), (SUPER, dh))
        beta = jnp.broadcast_to(_sigmoid(db_col), (SUPER, dh))
        g_row = jnp.broadcast_to(neg_a * softplus(dar_ref[0, 0, :, rows] + dtb), (8, SUPER))
        gc = sum(_dot(incl01, t) for t in _split3(g_col))
        tot = sum(_dot(same01, t) for t in _split3(g_col))
        gc_row = sum(_dot_nt(t, incl01) for t in _split3(g_row))[0:1]
        eg = jnp.exp(gc)
        ekd = jnp.exp(tot - gc)
        diff = jnp.concatenate([gc, gc], axis=1) - gc_row
        decay = jnp.exp(jnp.where(incl, diff, NEG_BIG))
        kb = kf * beta
        vb = v.astype(F32) * beta
        mm = jnp.where(strict, _dot_nt(kb.astype(BF16), k) * decay, 0.0)
        t_inv = eye - jnp.where(lvl == 0, mm, 0.0)
        for j in range(1, 6):
            tb = t_inv.astype(BF16)
            wj = _dot(tb, jnp.where(lvl == j, mm, 0.0).astype(BF16))
            t_inv = t_inv - _dot(wj.astype(BF16), tb)
        rhs = jnp.concatenate([kb * eg, vb], axis=1).astype(BF16)
        wu = _dot(t_inv.astype(BF16), rhs)
        wub = wu.astype(BF16)
        qk = jnp.where(incl, _dot_nt(q, k) * decay, 0.0)
        ab = _dot(qk.astype(BF16), wub)
        aq = (q.astype(F32) * eg - ab[:, :dh]).astype(BF16)
        bo = ab[:, dh:]
        kd = (kf * ekd).astype(BF16)
        nch = SUPER // CHUNK
        for ch in (range(nch - 1, -1, -1) if rev else range(nch)):
            cr = slice(ch * CHUNK, (ch + 1) * CHUNK)
            st = s_scr[...]
            stb = st.astype(BF16)
            o_ref[0, sc * SUPER + ch * CHUNK:sc * SUPER + (ch + 1) * CHUNK, :] = _dot(aq[cr], stb) + bo[cr]
            pq = _dot_tn(kd[cr], wub[cr])
            egl = jnp.exp(tot[ch * CHUNK:ch * CHUNK + 1, :])
            s_scr[...] = egl * st - _dot(pq[:, :dh].astype(BF16), stb) + pq[:, dh:]


def _gdn(qkv, da_col, db_col, da_row, a_log, dt_bias, rev):
    bsz, s, _ = qkv.shape
    dh = DN_HEAD_DIM
    ts = _tile(s, 1024)
    n = s // ts
    sidx = (lambda i: n - 1 - i) if rev else (lambda i: i)
    return pl.pallas_call(
        functools.partial(_gdn_kernel, ts=ts, rev=rev),
        grid=(bsz, DN_HEADS, n),
        in_specs=[pl.BlockSpec((1, ts, dh), lambda b, h, i: (b, sidx(i), h)),
                  pl.BlockSpec((1, ts, dh), lambda b, h, i: (b, sidx(i), DN_HEADS + h)),
                  pl.BlockSpec((1, ts, dh), lambda b, h, i: (b, sidx(i), 2 * DN_HEADS + h)),
                  pl.BlockSpec((1, ts, DN_HEADS), lambda b, h, i: (b, sidx(i), 0)),
                  pl.BlockSpec((1, ts, DN_HEADS), lambda b, h, i: (b, sidx(i), 0)),
                  pl.BlockSpec((1, 1, 1, ts), lambda b, h, i: (b, h, 0, sidx(i))),
                  pl.BlockSpec((1, 1, 1), lambda b, h, i: (h, 0, 0)),
                  pl.BlockSpec((1, 1, 1), lambda b, h, i: (h, 0, 0))],
        out_specs=pl.BlockSpec((1, ts, dh), lambda b, h, i: (b, sidx(i), h)),
        out_shape=jax.ShapeDtypeStruct((bsz, s, DN_HEADS * dh), F32),
        scratch_shapes=[pltpu.VMEM((dh, dh), F32)],
        compiler_params=_cparams("parallel", "parallel", "arbitrary"),
        name="gdn_bwd" if rev else "gdn_fwd",
    )(qkv, qkv, qkv, da_col, db_col, da_row, a_log.reshape(DN_HEADS, 1, 1), dt_bias.reshape(DN_HEADS, 1, 1))


def _gdn_out_kernel(of_ref, ob_ref, z_ref, w_ref, o_ref):
    dh = DN_HEAD_DIM
    w = w_ref[...]
    for hh in range(DN_HEADS):
        cs = slice(hh * dh, (hh + 1) * dh)
        o = of_ref[0, :, cs] + ob_ref[0, :, cs]
        z = z_ref[0, :, cs].astype(F32)
        y = o * lax.rsqrt(jnp.mean(o * o, axis=1, keepdims=True) + RMS_EPS) * w
        o_ref[0, :, cs] = (y * (z * _sigmoid(z))).astype(o_ref.dtype)


def _gdn_out(o_f, o_b, dn, norm_w):
    bsz, s, cw = o_f.shape
    ts = _tile(s, 256)
    row = pl.BlockSpec((1, ts, cw), lambda b, i: (b, i, 0))
    return pl.pallas_call(
        _gdn_out_kernel,
        grid=(bsz, s // ts),
        in_specs=[row, row,
                  pl.BlockSpec((1, ts, cw), lambda b, i: (b, i, 3)),
                  pl.BlockSpec((1, DN_HEAD_DIM), lambda b, i: (0, 0))],
        out_specs=row,
        out_shape=jax.ShapeDtypeStruct((bsz, s, cw), BF16),
        compiler_params=_cparams("parallel", "parallel"),
        name="gdn_out",
    )(o_f, o_b, dn, norm_w.reshape(1, DN_HEAD_DIM))


def _merge_kernel(oa_ref, od_ref, wa_ref, wd_ref, ga_ref, gd_ref, o_ref):
    a = _dot(oa_ref[...], wa_ref[...])
    d = _dot(od_ref[...], wd_ref[...])
    o_ref[...] = (_sigmoid(ga_ref[...].astype(F32)) * a + _sigmoid(gd_ref[...].astype(F32)) * d).astype(o_ref.dtype)


def _merge(oa, od, wa, wd, gates):
    m, k = oa.shape
    n = wa.shape[1]
    tm, tn = _tile(m, 1024), _tile(n, 1024)
    nj = n // tn
    return pl.pallas_call(
        _merge_kernel,
        grid=(m // tm, nj),
        in_specs=[pl.BlockSpec((tm, k), lambda i, j: (i, 0)),
                  pl.BlockSpec((tm, k), lambda i, j: (i, 0)),
                  pl.BlockSpec((k, tn), lambda i, j: (0, j)),
                  pl.BlockSpec((k, tn), lambda i, j: (0, j)),
                  pl.BlockSpec((tm, tn), lambda i, j: (i, j)),
                  pl.BlockSpec((tm, tn), lambda i, j: (i, nj + j))],
        out_specs=pl.BlockSpec((tm, tn), lambda i, j: (i, j)),
        out_shape=jax.ShapeDtypeStruct((m, n), BF16),
        compiler_params=_cparams("parallel", "arbitrary"),
        name="branch_merge",
    )(oa, od, wa, wd, gates, gates)


def _peer_route(q, subkeys):
    t = q.shape[0]
    q4 = q.reshape(t, PEER_HEADS, 2, -1)
    s = jnp.einsum('thpd,hpnd->thpn', q4, subkeys, preferred_element_type=F32)
    sv, si = lax.top_k(s, PEER_TOPK)
    cand = (sv[:, :, 0, :, None] + sv[:, :, 1, None, :]).reshape(t, PEER_HEADS, PEER_TOPK * PEER_TOPK)
    fv, fi = lax.top_k(cand, PEER_TOPK)
    i1 = jnp.take_along_axis(si[:, :, 0], fi // PEER_TOPK, axis=-1)
    i2 = jnp.take_along_axis(si[:, :, 1], fi % PEER_TOPK, axis=-1)
    idx = (i1 * N_KEYS + i2).reshape(t, PEER_HEADS * PEER_TOPK)
    gate = jax.nn.softmax(fv, axis=-1).reshape(t, PEER_HEADS * PEER_TOPK)
    return idx.astype(jnp.int32), gate


def _gate_kernel(idx_ref, g_ref, o_ref, *, tt):
    ne = idx_ref.shape[1]
    sub = lax.broadcasted_iota(jnp.int32, (N_KEYS, ne), 0)

    def body(t, carry):
        idx = idx_ref[pl.ds(t, 1), :]
        g = g_ref[pl.ds(t, 1), :]
        i1 = lax.shift_right_logical(idx, 7)
        i2 = lax.bitwise_and(idx, N_KEYS - 1)
        a = jnp.where(sub == i1, g, 0.0).astype(BF16)
        bt = jnp.where(sub == i2, 1.0, 0.0).astype(BF16)
        o_ref[t] = _dot_nt(a, bt).astype(o_ref.dtype)
        return carry

    lax.fori_loop(0, tt, body, 0)


def _peer_gates(idx, gate):
    t, ne = idx.shape
    tt = _tile(t, 128)
    return pl.pallas_call(
        functools.partial(_gate_kernel, tt=tt),
        grid=(t // tt,),
        in_specs=[pl.BlockSpec((tt, ne), lambda i: (i, 0)),
                  pl.BlockSpec((tt, ne), lambda i: (i, 0))],
        out_specs=pl.BlockSpec((tt, N_KEYS, N_KEYS), lambda i: (i, 0, 0)),
        out_shape=jax.ShapeDtypeStruct((t, N_KEYS, N_KEYS), BF16),
        compiler_params=_cparams("parallel"),
        name="peer_gates",
    )(idx, gate)


def _peer_kernel(x_ref, u_ref, v_ref, g_ref, o_ref):
    n = pl.program_id(1)

    @pl.when(n == 0)
    def _():
        o_ref[...] = jnp.zeros(o_ref.shape, F32)

    hid = _dot_nt(x_ref[...], u_ref[...])
    act = 0.5 * hid * (1.0 + lax.erf(hid * (2.0 ** -0.5)))
    wgt = (g_ref[...].astype(F32) * act).astype(BF16)
    o_ref[...] += _dot(wgt, v_ref[...])


def _peer_dense(x, u, v, g):
    t, d = x.shape
    ne = u.shape[0]
    tm, tn = _tile(t, 512), _tile(ne, 512)
    return pl.pallas_call(
        _peer_kernel,
        grid=(t // tm, ne // tn),
        in_specs=[pl.BlockSpec((tm, d), lambda i, n: (i, 0)),
                  pl.BlockSpec((tn, d), lambda i, n: (n, 0)),
                  pl.BlockSpec((tn, d), lambda i, n: (n, 0)),
                  pl.BlockSpec((tm, tn), lambda i, n: (i, n))],
        out_specs=pl.BlockSpec((tm, d), lambda i, n: (i, 0)),
        out_shape=jax.ShapeDtypeStruct((t, d), F32),
        compiler_params=_cparams("parallel", "arbitrary"),
        name="peer_dense",
    )(x, u, v, g)


def _token_mixer(h, lam_init, w_in, lambda_qk, attn_subln_w, dn_conv_w, dn_a_log, dn_dt_bias, dn_norm_w,
                 w_br_attn, w_br_dn, w_out):
    bsz, s, d = h.shape
    t = bsz * s
    aw = ATTN_HEADS * 2 * ATTN_HEAD_DIM
    dw = DN_HEADS * DN_HEAD_DIM
    h2d = h.reshape(t, d)

    o0, o1, o3 = 3 * aw, 3 * aw + 4 * dw, 3 * aw + 4 * dw + 4 * DN_HEADS
    w_attn = w_in[:, :o0].astype(BF16)
    w_dn = w_in[:, o0:o1].astype(BF16)
    w_ab = jnp.zeros((d, LANES), BF16).at[:, :4 * DN_HEADS].set(w_in[:, o1:o3].astype(BF16))
    w_gate = w_in[:, o3:].astype(BF16)
    qkv = _matmul(h2d, w_attn, BF16, "proj_attn").reshape(bsz, s, 3 * aw)
    dn = _matmul(h2d, w_dn, BF16, "proj_dn").reshape(bsz, s, 4 * dw)
    dab = _matmul(h2d, w_ab, F32, "proj_ab").reshape(bsz, s, LANES)
    gates = _matmul(h2d, w_gate, BF16, "proj_gate")

    oa = _diff_attention(qkv, lambda_qk, attn_subln_w, lam_init)

    dqkv = _dn_conv(dn, dn_conv_w)
    o_dirs = []
    for dr in range(2):
        da_col = dab[:, :, dr * DN_HEADS:(dr + 1) * DN_HEADS]
        db_col = dab[:, :, (2 + dr) * DN_HEADS:(3 + dr) * DN_HEADS]
        da_row = jnp.transpose(da_col, (0, 2, 1)).reshape(bsz, DN_HEADS, 1, s)
        o_dirs.append(_gdn(dqkv, da_col, db_col, da_row, dn_a_log[dr], dn_dt_bias[dr], rev=bool(dr)))
    od = _gdn_out(o_dirs[0], o_dirs[1], dn, dn_norm_w)

    merged = _merge(oa.reshape(t, aw), od.reshape(t, dw), w_br_attn.astype(BF16), w_br_dn.astype(BF16), gates)
    return _matmul(merged, w_out.astype(BF16), F32, "out_proj").reshape(bsz, s, d)


def _peer_ffn(h, peer_wq, peer_subkeys, peer_u, peer_v):
    bsz, s, d = h.shape
    t = bsz * s
    h2d = h.reshape(t, d)
    pq = _matmul(h2d, peer_wq.astype(BF16), F32, "peer_q")
    idx, gate = _peer_route(pq, peer_subkeys)
    gmat = _peer_gates(idx, gate).reshape(t, N_KEYS * N_KEYS)
    return _peer_dense(h2d, peer_u.astype(BF16), peer_v.astype(BF16), gmat).reshape(bsz, s, d)


def kernel(x_prompt, x_sample, c_prompt, c_sample, w_ada, b_ada, w_in, lambda_qk, attn_subln_w, dn_conv_w, dn_a_log, dn_dt_bias, dn_norm_w, w_br_attn, w_br_dn, w_out, ln1_g, ln1_b, peer_wq, peer_subkeys, peer_u, peer_v, ln2_g, ln2_b):
    depth = w_ada.shape[0]
    alpha = (2.0 * depth) ** 0.25
    nb_prompt = x_prompt.shape[0]
    x = jnp.concatenate([x_prompt, x_sample], axis=0)
    c = jnp.concatenate([c_prompt, c_sample], axis=0)
    bsz, s, d = x.shape
    c8 = jnp.zeros((8, d), F32).at[:bsz].set(c)

    mods = []
    for l in range(depth):
        mod = _ada(c8, w_ada[l], b_ada[l])[:bsz]
        mods.append([mod[:, None, j * d:(j + 1) * d] for j in range(6)])

    h = _lnmod(x, mods[0][1], mods[0][0])
    for l in range(depth):
        sh1, sc1, g1, sh2, sc2, g2 = mods[l]
        lam_init = 0.8 - 0.6 * math.exp(-0.3 * l)
        y = _token_mixer(h, lam_init, w_in[l], lambda_qk[l], attn_subln_w[l], dn_conv_w[l], dn_a_log[l],
                         dn_dt_bias[l], dn_norm_w[l], w_br_attn[l], w_br_dn[l], w_out[l])
        x, h = _res_ln(x, y, g1, ln1_g[l], ln1_b[l], alpha, sc2, sh2)
        y = _peer_ffn(h, peer_wq[l], peer_subkeys[l], peer_u[l], peer_v[l])
        if l + 1 < depth:
            x, h = _res_ln(x, y, g2, ln2_g[l], ln2_b[l], alpha, mods[l + 1][1], mods[l + 1][0])
        else:
            x, _ = _res_ln(x, y, g2, ln2_g[l], ln2_b[l], alpha)

    return x[:nb_prompt], x[nb_prompt:]
```

```python
import functools
import math

import jax
import jax.numpy as jnp
from jax import lax
from jax.experimental import pallas as pl
from jax.experimental.pallas import tpu as pltpu

F32 = jnp.float32
BF16 = jnp.bfloat16

V7X_VMEM_BYTES = 64 * 1024 * 1024
VMEM_LIMIT = V7X_VMEM_BYTES - 8 * 1024 * 1024
LANES = 128

ATTN_HEADS = 8
ATTN_HEAD_DIM = 128
DN_HEADS = 16
DN_HEAD_DIM = 128
CONV_K = 5
CHUNK = 64
SUPER = 256
GDN_HG = 2
PEER_HEADS = 8
N_KEYS = 128
PEER_TOPK = 16
LN_EPS = 1e-5
RMS_EPS = 1e-6
LOG2E = 1.4426950408889634
NEG_BIG = -1e30


def _cparams(*sem):
    return pltpu.CompilerParams(dimension_semantics=sem, vmem_limit_bytes=VMEM_LIMIT)


def _tile(n, pref):
    t = min(n, pref)
    assert n % t == 0, (n, pref)
    return t


def _dot(a, b):
    return jnp.dot(a, b, preferred_element_type=F32)


def _dot_nt(a, b):
    return lax.dot_general(a, b, (((1,), (1,)), ((), ())), preferred_element_type=F32)


def _dot_tn(a, b):
    return lax.dot_general(a, b, (((0,), (0,)), ((), ())), preferred_element_type=F32)


def _sigmoid(x):
    return 1.0 / (1.0 + jnp.exp(-x))


def _layer_norm(x):
    mu = jnp.mean(x, axis=-1, keepdims=True)
    xc = x - mu
    var = jnp.mean(xc * xc, axis=-1, keepdims=True)
    return xc * lax.rsqrt(var + LN_EPS)


def _mm_kernel(a_ref, b_ref, o_ref):
    o_ref[...] = _dot(a_ref[...], b_ref[...]).astype(o_ref.dtype)


def _matmul(a, b, out_dtype, name, tm=1024, tn=1024):
    m, k = a.shape
    n = b.shape[1]
    tm, tn = _tile(m, tm), _tile(n, tn)
    return pl.pallas_call(
        _mm_kernel,
        grid=(m // tm, n // tn),
        in_specs=[pl.BlockSpec((tm, k), lambda i, j: (i, 0)),
                  pl.BlockSpec((k, tn), lambda i, j: (0, j))],
        out_specs=pl.BlockSpec((tm, tn), lambda i, j: (i, j)),
        out_shape=jax.ShapeDtypeStruct((m, n), out_dtype),
        compiler_params=_cparams("parallel", "arbitrary"),
        name=name,
    )(a, b)


def _ada_kernel(c_ref, w_ref, b_ref, o_ref):
    c = c_ref[...]
    s = (c * _sigmoid(c)).astype(BF16)
    o_ref[...] = _dot(s, w_ref[...].astype(BF16)) + b_ref[...]


def _ada(c8, w, b):
    d, n = w.shape
    tn = _tile(n, 512)
    return pl.pallas_call(
        _ada_kernel,
        grid=(n // tn,),
        in_specs=[pl.BlockSpec((8, d), lambda j: (0, 0)),
                  pl.BlockSpec((d, tn), lambda j: (0, j)),
                  pl.BlockSpec((1, tn), lambda j: (0, j))],
        out_specs=pl.BlockSpec((8, tn), lambda j: (0, j)),
        out_shape=jax.ShapeDtypeStruct((8, n), F32),
        compiler_params=_cparams("arbitrary"),
        name="ada_mod",
    )(c8, w, b.reshape(1, n))


def _lnmod_kernel(x_ref, sc_ref, sh_ref, h_ref):
    h_ref[0] = (_layer_norm(x_ref[0]) * (1.0 + sc_ref[0]) + sh_ref[0]).astype(h_ref.dtype)


def _lnmod(x, sc, sh):
    bsz, s, d = x.shape
    tr = _tile(s, 256)
    row = pl.BlockSpec((1, tr, d), lambda b, i: (b, i, 0))
    vec = pl.BlockSpec((1, 1, d), lambda b, i: (b, 0, 0))
    return pl.pallas_call(
        _lnmod_kernel,
        grid=(bsz, s // tr),
        in_specs=[row, vec, vec],
        out_specs=row,
        out_shape=jax.ShapeDtypeStruct((bsz, s, d), BF16),
        compiler_params=_cparams("parallel", "parallel"),
        name="ln_mod",
    )(x, sc, sh)


def _res_ln_kernel(x_ref, y_ref, g_ref, lg_ref, lb_ref, *rest, alpha, with_h):
    z = alpha * x_ref[0] + g_ref[0] * y_ref[0].astype(F32)
    x1 = _layer_norm(z) * lg_ref[...] + lb_ref[...]
    if with_h:
        sc_ref, sh_ref, xo_ref, h_ref = rest
        h_ref[0] = (_layer_norm(x1) * (1.0 + sc_ref[0]) + sh_ref[0]).astype(h_ref.dtype)
    else:
        (xo_ref,) = rest
    xo_ref[0] = x1


def _res_ln(x, y, gate, ln_g, ln_b, alpha, sc=None, sh=None):
    bsz, s, d = x.shape
    tr = _tile(s, 256)
    row = pl.BlockSpec((1, tr, d), lambda b, i: (b, i, 0))
    vec = pl.BlockSpec((1, 1, d), lambda b, i: (b, 0, 0))
    par = pl.BlockSpec((1, d), lambda b, i: (0, 0))
    with_h = sc is not None
    ins = [x, y, gate, ln_g.reshape(1, d), ln_b.reshape(1, d)]
    in_specs = [row, row, vec, par, par]
    out_shape = [jax.ShapeDtypeStruct((bsz, s, d), F32)]
    out_specs = [row]
    if with_h:
        ins += [sc, sh]
        in_specs += [vec, vec]
        out_shape.append(jax.ShapeDtypeStruct((bsz, s, d), BF16))
        out_specs.append(row)
    out = pl.pallas_call(
        functools.partial(_res_ln_kernel, alpha=alpha, with_h=with_h),
        grid=(bsz, s // tr),
        in_specs=in_specs,
        out_specs=out_specs,
        out_shape=out_shape,
        compiler_params=_cparams("parallel", "parallel"),
        name="res_ln",
    )(*ins)
    return (out[0], out[1]) if with_h else (out[0], None)


ATTN_SKIP_LOG2 = 130.0
ATTN_REDO_LOG2 = 60.0


def _attn_kernel(lq_ref, w_ref, q_ref, k_ref, v_ref, o_ref, m_scr, l_scr, acc_scr, kn_scr, *, tq, tk, lam_init):
    dh = ATTN_HEAD_DIM
    h = pl.program_id(1)
    i = pl.program_id(2)
    s_len = k_ref.shape[1]
    nk = s_len // tk
    slope2 = jnp.exp2(-(jnp.full((1, 1), h + 1, jnp.int32).astype(F32))) * LOG2E
    c1 = dh ** -0.5 * LOG2E
    q0 = i * tq
    jd = lax.div(q0, tk)
    q = q_ref[0]
    qrel = lax.broadcasted_iota(jnp.int32, (tq, 1), 0).astype(F32)
    krel = lax.broadcasted_iota(jnp.int32, (1, tk), 1).astype(F32)

    @pl.when(i == 0)
    def _():
        def kn_body(j, mx):
            kk = k_ref[0, pl.ds(pl.multiple_of(j * tk, tk), tk), :].astype(F32)
            kk = kk * kk
            n2 = jnp.maximum(jnp.sum(kk[:, :dh], axis=1, keepdims=True), jnp.sum(kk[:, dh:], axis=1, keepdims=True))
            return jnp.maximum(mx, jnp.max(n2, axis=0, keepdims=True))
        kn2 = lax.fori_loop(0, nk, kn_body, jnp.zeros((1, 1), F32))
        kn_scr[...] = jnp.broadcast_to(kn2, kn_scr.shape)

    qf = q.astype(F32)
    kdg = k_ref[0, pl.ds(pl.multiple_of(q0, tq), tq), :].astype(F32)
    sii = [jnp.sum(qf[:, c * dh:(c + 1) * dh] * kdg[:, c * dh:(c + 1) * dh], axis=1, keepdims=True) * c1
           for c in range(2)]
    qq = qf * qf
    qn2 = jnp.max(jnp.maximum(jnp.sum(qq[:, :dh], axis=1, keepdims=True), jnp.sum(qq[:, dh:], axis=1, keepdims=True)),
                  axis=0, keepdims=True)
    sii_min = jnp.min(jnp.minimum(sii[0], sii[1]), axis=0, keepdims=True)
    bound = jnp.sqrt(qn2 * kn_scr[0:1, 0:1]) * c1 - sii_min
    reach = jnp.minimum((bound + ATTN_SKIP_LOG2) / (slope2 * tk), float(nk))
    nd = jnp.max(jnp.floor(reach).astype(jnp.int32)) + 1
    jlo = jnp.maximum(jd - nd, 0)
    jhi = jnp.minimum(jd + nd, nk - 1)

    def load_kv(j):
        start = pl.multiple_of(j * tk, tk)
        return k_ref[0, pl.ds(start, tk), :], v_ref[0, pl.ds(start, tk), :]

    def dist_bias(j):
        return slope2 * jnp.abs((qrel + (q0 - j * tk).astype(F32)) - krel)

    l_scr[...] = jnp.zeros(l_scr.shape, F32)
    acc_scr[...] = jnp.zeros(acc_scr.shape, F32)

    def accumulate(k, v, logit_fn):
        for c in range(2):
            s = _dot_nt(q[:, c * dh:(c + 1) * dh], k[:, c * dh:(c + 1) * dh])
            p = jnp.exp2(logit_fn(c, s))
            l_scr[c] += jnp.sum(p, axis=1, keepdims=True)
            acc_scr[c] += _dot(p.astype(BF16), v)

    row_left = [-slope2 * qrel - sii[c] for c in range(2)]
    row_right = [slope2 * qrel - sii[c] for c in range(2)]

    def left_body(j, carry):
        k, v = load_kv(j)
        col = slope2 * (krel - (q0 - j * tk).astype(F32))
        accumulate(k, v, lambda c, s: s * c1 + col + row_left[c])
        return carry

    def right_body(j, carry):
        k, v = load_kv(j)
        col = -slope2 * (krel + (j * tk - q0).astype(F32))
        accumulate(k, v, lambda c, s: s * c1 + col + row_right[c])
        return carry

    lax.fori_loop(jlo, jd, left_body, 0)
    k_diag, v_diag = load_kv(jd)
    bias_diag = dist_bias(jd)
    accumulate(k_diag, v_diag, lambda c, s: s * c1 - bias_diag - sii[c])
    lax.fori_loop(jd + 1, jhi + 1, right_body, 0)

    bad = jnp.maximum(jnp.max(jnp.where(l_scr[...] <= 2.0 ** ATTN_REDO_LOG2, 0.0, 1.0)),
                      jnp.max(jnp.where(jnp.isfinite(acc_scr[...]), 0.0, 1.0)))

    @pl.when(bad > 0.0)
    def _():
        m_scr[...] = jnp.full(m_scr.shape, NEG_BIG, F32)
        l_scr[...] = jnp.zeros(l_scr.shape, F32)
        acc_scr[...] = jnp.zeros(acc_scr.shape, F32)

        def body(j, carry):
            k, v = load_kv(j)
            bias = dist_bias(j)
            for c in range(2):
                s = _dot_nt(q[:, c * dh:(c + 1) * dh], k[:, c * dh:(c + 1) * dh]) * c1 - bias
                m_old = m_scr[c]
                m_new = jnp.maximum(m_old, jnp.max(s, axis=1, keepdims=True))
                alpha = jnp.exp2(m_old - m_new)
                p = jnp.exp2(s - m_new[:, :1])
                l_scr[c] = alpha * l_scr[c] + jnp.sum(p, axis=1, keepdims=True)
                acc_scr[c] = alpha[:, :1] * acc_scr[c] + _dot(p.astype(BF16), v)
                m_scr[c] = m_new
            return carry

        lax.fori_loop(jlo, jhi + 1, body, 0)

    lq = lq_ref[...]
    lam = (jnp.exp(jnp.sum(lq[0:1] * lq[1:2], axis=1, keepdims=True))
           - jnp.exp(jnp.sum(lq[2:3] * lq[3:4], axis=1, keepdims=True)) + lam_init)
    o = acc_scr[0] / l_scr[0][:, :1] - lam * (acc_scr[1] / l_scr[1][:, :1])
    ms = jnp.mean(o * o, axis=1, keepdims=True)
    o_ref[0] = (o * lax.rsqrt(ms + RMS_EPS) * w_ref[...] * (1.0 - lam_init)).astype(o_ref.dtype)


def _diff_attention(qkv, lambda_qk, subln_w, lam_init):
    bsz, s, _ = qkv.shape
    hw = 2 * ATTN_HEAD_DIM
    tq = _tile(s, 512)
    tk = _tile(s, 1024)
    assert tk % tq == 0
    return pl.pallas_call(
        functools.partial(_attn_kernel, tq=tq, tk=tk, lam_init=lam_init),
        grid=(bsz, ATTN_HEADS, s // tq),
        in_specs=[pl.BlockSpec((4, ATTN_HEAD_DIM), lambda b, h, i: (0, 0)),
                  pl.BlockSpec((1, hw), lambda b, h, i: (0, 0)),
                  pl.BlockSpec((1, tq, hw), lambda b, h, i: (b, i, h)),
                  pl.BlockSpec((1, s, hw), lambda b, h, i: (b, 0, ATTN_HEADS + h)),
                  pl.BlockSpec((1, s, hw), lambda b, h, i: (b, 0, 2 * ATTN_HEADS + h))],
        out_specs=pl.BlockSpec((1, tq, hw), lambda b, h, i: (b, i, h)),
        out_shape=jax.ShapeDtypeStruct((bsz, s, ATTN_HEADS * hw), BF16),
        scratch_shapes=[pltpu.VMEM((2, tq, LANES), F32),
                        pltpu.VMEM((2, tq, LANES), F32),
                        pltpu.VMEM((2, tq, hw), F32),
                        pltpu.VMEM((8, LANES), F32)],
        compiler_params=_cparams("parallel", "parallel", "arbitrary"),
        name="diff_attn",
    )(lambda_qk, subln_w.reshape(1, hw), qkv, qkv, qkv)


def _conv_kernel(prev_ref, cur_ref, next_ref, w_ref, o_ref, *, ts, halo):
    i = pl.program_id(1)
    p = pl.program_id(2)
    ns = pl.num_programs(1)
    dh = DN_HEAD_DIM
    x = cur_ref[0].astype(F32)
    pv = jnp.where(i > 0, prev_ref[0].astype(F32), 0.0)
    nx = jnp.where(i < ns - 1, next_ref[0].astype(F32), 0.0)
    w = w_ref[...]
    row = lax.broadcasted_iota(jnp.int32, (ts, 1), 0)
    xm1 = jnp.where(row == 0, pv[halo - 1:halo], pltpu.roll(x, 1, 0))
    xm2 = jnp.where(row == 0, pv[halo - 2:halo - 1], jnp.where(row == 1, pv[halo - 1:halo], pltpu.roll(x, 2, 0)))
    xp1 = jnp.where(row == ts - 1, nx[0:1], pltpu.roll(x, ts - 1, 0))
    xp2 = jnp.where(row == ts - 2, nx[0:1], jnp.where(row == ts - 1, nx[1:2], pltpu.roll(x, ts - 2, 0)))
    y = w[0:1] * xm2 + w[1:2] * xm1 + w[2:3] * x + w[3:4] * xp1 + w[4:5] * xp2
    y = y * _sigmoid(y)
    qscale = jnp.where(p == 0, dh ** -0.5, 1.0)
    for hh in range(y.shape[1] // dh):
        seg = y[:, hh * dh:(hh + 1) * dh]
        nrm = lax.rsqrt(jnp.sum(seg * seg, axis=1, keepdims=True) + RMS_EPS) * qscale
        fac = jnp.where(p == 2, 1.0, nrm)
        o_ref[0, :, hh * dh:(hh + 1) * dh] = (seg * fac).astype(o_ref.dtype)


def _dn_conv(dn, conv_w):
    bsz, s, _ = dn.shape
    cw = DN_HEADS * DN_HEAD_DIM
    ts = _tile(s, 256)
    halo = 16
    hb = ts // halo
    nhb = s // halo
    return pl.pallas_call(
        functools.partial(_conv_kernel, ts=ts, halo=halo),
        grid=(bsz, s // ts, 3),
        in_specs=[pl.BlockSpec((1, halo, cw), lambda b, i, p: (b, jnp.maximum(i * hb - 1, 0), p)),
                  pl.BlockSpec((1, ts, cw), lambda b, i, p: (b, i, p)),
                  pl.BlockSpec((1, halo, cw), lambda b, i, p: (b, jnp.minimum((i + 1) * hb, nhb - 1), p)),
                  pl.BlockSpec((CONV_K, cw), lambda b, i, p: (0, p))],
        out_specs=pl.BlockSpec((1, ts, cw), lambda b, i, p: (b, i, p)),
        out_shape=jax.ShapeDtypeStruct((bsz, s, 3 * cw), BF16),
        compiler_params=_cparams("parallel", "parallel", "parallel"),
        name="dn_conv",
    )(dn, dn, dn, conv_w)


def _split3(x):
    hi = x.astype(BF16)
    r1 = x - hi.astype(F32)
    mid = r1.astype(BF16)
    lo = (r1 - mid.astype(F32)).astype(BF16)
    return hi, mid, lo


def _softplus(x):
    return jnp.maximum(x, 0.0) + jnp.log1p(jnp.exp(-jnp.abs(x)))


def _gdn_masks(rev):
    r = lax.broadcasted_iota(jnp.int32, (SUPER, SUPER), 0)
    c = lax.broadcasted_iota(jnp.int32, (SUPER, SUPER), 1)
    same = (r // CHUNK) == (c // CHUNK)
    incl = jnp.logical_and(same, (c >= r) if rev else (c <= r))
    strict = jnp.logical_and(same, (c > r) if rev else (c < r))
    return dict(
        incl=incl, strict=strict,
        incl01=jnp.where(incl, 1.0, 0.0).astype(BF16),
        same01=jnp.where(same, 1.0, 0.0).astype(BF16),
        eye=jnp.where(r == c, 1.0, 0.0),
        lvl=31 - lax.clz(lax.bitwise_xor(r, c)))


def _gdn_superchunks(ps):
    dh = DN_HEAD_DIM
    n = range(len(ps))
    mk = [p["mk"] for p in ps]
    kf = [p["k"].astype(F32) for p in ps]
    g_col = [jnp.broadcast_to(p["neg_a"] * _softplus(p["da_col"] + p["dtb"]), (SUPER, dh)) for p in ps]
    beta = [jnp.broadcast_to(_sigmoid(p["db_col"]), (SUPER, dh)) for p in ps]
    g_row = [jnp.broadcast_to(p["neg_a"] * _softplus(p["da_row"] + p["dtb"]), (8, SUPER)) for p in ps]
    g_col3 = [_split3(g) for g in g_col]
    g_row3 = [_split3(g) for g in g_row]
    gc = [sum(_dot(mk[i]["incl01"], t) for t in g_col3[i]) for i in n]
    tot = [sum(_dot(mk[i]["same01"], t) for t in g_col3[i]) for i in n]
    gc_row = [sum(_dot_nt(t, mk[i]["incl01"]) for t in g_row3[i])[0:1] for i in n]
    eg = [jnp.exp(g) for g in gc]
    ekd = [jnp.exp(tot[i] - gc[i]) for i in n]
    decay = [jnp.exp(jnp.where(mk[i]["incl"], jnp.concatenate([gc[i], gc[i]], axis=1) - gc_row[i], NEG_BIG))
             for i in n]
    kb = [kf[i] * beta[i] for i in n]
    vb = [ps[i]["v"].astype(F32) * beta[i] for i in n]
    mm = [jnp.where(mk[i]["strict"], _dot_nt(kb[i].astype(BF16), ps[i]["k"]) * decay[i], 0.0) for i in n]
    t_inv = [mk[i]["eye"] - jnp.where(mk[i]["lvl"] == 0, mm[i], 0.0) for i in n]
    for j in range(1, 6):
        tb = [t.astype(BF16) for t in t_inv]
        wj = [_dot(tb[i], jnp.where(mk[i]["lvl"] == j, mm[i], 0.0).astype(BF16)) for i in n]
        t_inv = [t_inv[i] - _dot(wj[i].astype(BF16), tb[i]) for i in n]
    rhs = [jnp.concatenate([kb[i] * eg[i], vb[i]], axis=1).astype(BF16) for i in n]
    wub = [_dot(t_inv[i].astype(BF16), rhs[i]).astype(BF16) for i in n]
    qk = [jnp.where(mk[i]["incl"], _dot_nt(ps[i]["q"], ps[i]["k"]) * decay[i], 0.0) for i in n]
    ab = [_dot(qk[i].astype(BF16), wub[i]) for i in n]
    aq = [(ps[i]["q"].astype(F32) * eg[i] - ab[i][:, :dh]).astype(BF16) for i in n]
    kd = [(kf[i] * ekd[i]).astype(BF16) for i in n]
    nch = SUPER // CHUNK
    for step in range(nch):
        for i in n:
            p = ps[i]
            ch = nch - 1 - step if p["rev"] else step
            cr = slice(ch * CHUNK, (ch + 1) * CHUNK)
            st = p["s"][...]
            stb = st.astype(BF16)
            orow = slice(p["row0"] + ch * CHUNK, p["row0"] + (ch + 1) * CHUNK)
            p["o"][0, orow, p["col0"]:p["col0"] + dh] = _dot(aq[i][cr], stb) + ab[i][cr, dh:]
            pq = _dot_tn(kd[i][cr], wub[i][cr])
            egl = jnp.exp(tot[i][ch * CHUNK:ch * CHUNK + 1, :])
            p["s"][...] = egl * st - _dot(pq[:, :dh].astype(BF16), stb) + pq[:, dh:]


def _gdn_kernel(qf_ref, kf_ref, vf_ref, qb_ref, kb_ref, vb_ref, cf_ref, cb_ref, rf_ref, rb_ref, al_ref, dtb_ref,
                of_ref, ob_ref, s_scr, *, ts):
    dh = DN_HEAD_DIM
    hg = pl.program_id(1)
    nsc = ts // SUPER

    @pl.when(pl.program_id(2) == 0)
    def _():
        s_scr[...] = jnp.zeros(s_scr.shape, F32)

    lane = lax.broadcasted_iota(jnp.int32, (1, LANES), 1)
    chains = []
    for d, (q_ref, k_ref, v_ref, c_ref, r_ref, o_ref) in enumerate(
            ((qf_ref, kf_ref, vf_ref, cf_ref, rf_ref, of_ref), (qb_ref, kb_ref, vb_ref, cb_ref, rb_ref, ob_ref))):
        mk = _gdn_masks(bool(d))
        for hh in range(GDN_HG):
            h = hg * GDN_HG + hh
            chains.append(dict(q=q_ref, k=k_ref, v=v_ref, c=c_ref, r=r_ref, o=o_ref, mk=mk, d=d, hh=hh,
                               neg_a=-jnp.exp(al_ref[d, hh]), dtb=dtb_ref[d, hh],
                               asel=lane == d * DN_HEADS + h, bsel=lane == (2 + d) * DN_HEADS + h))

    for step in range(nsc):
        ps = []
        for cc in chains:
            d, hh = cc["d"], cc["hh"]
            sc = nsc - 1 - step if d else step
            rows = slice(sc * SUPER, (sc + 1) * SUPER)
            hc = slice(hh * dh, (hh + 1) * dh)
            cols = cc["c"][0, rows, :]
            ps.append(dict(q=cc["q"][0, rows, hc], k=cc["k"][0, rows, hc], v=cc["v"][0, rows, hc],
                           da_col=jnp.sum(jnp.where(cc["asel"], cols, 0.0), axis=1, keepdims=True),
                           db_col=jnp.sum(jnp.where(cc["bsel"], cols, 0.0), axis=1, keepdims=True),
                           da_row=cc["r"][0, hh, :, rows], neg_a=cc["neg_a"], dtb=cc["dtb"], mk=cc["mk"],
                           s=s_scr.at[d * GDN_HG + hh], o=cc["o"], row0=sc * SUPER, col0=hh * dh, rev=bool(d)))
        _gdn_superchunks(ps)


def _gdn(qkv, dab, da_row, a_log, dt_bias):
    bsz, s, _ = qkv.shape
    dh = DN_HEAD_DIM
    ts = _tile(s, 1024)
    n = s // ts

    def spec(shape, fn):
        return (pl.BlockSpec(shape, lambda b, h, i: fn(b, h, i)),
                pl.BlockSpec(shape, lambda b, h, i: fn(b, h, n - 1 - i)))

    ng = DN_HEADS // GDN_HG
    gw = GDN_HG * dh
    qs = spec((1, ts, gw), lambda b, h, t: (b, t, h))
    ks = spec((1, ts, gw), lambda b, h, t: (b, t, ng + h))
    vs = spec((1, ts, gw), lambda b, h, t: (b, t, 2 * ng + h))
    cs = spec((1, ts, LANES), lambda b, h, t: (b, t, 0))
    rf = pl.BlockSpec((1, GDN_HG, 1, ts), lambda b, h, i: (b, h, 0, i))
    rb = pl.BlockSpec((1, GDN_HG, 1, ts), lambda b, h, i: (b, ng + h, 0, n - 1 - i))
    par = pl.BlockSpec((2, GDN_HG, 1, 1), lambda b, h, i: (0, h, 0, 0))
    out = jax.ShapeDtypeStruct((bsz, s, DN_HEADS * dh), F32)
    return pl.pallas_call(
        functools.partial(_gdn_kernel, ts=ts),
        grid=(bsz, ng, n),
        in_specs=[qs[0], ks[0], vs[0], qs[1], ks[1], vs[1], cs[0], cs[1], rf, rb, par, par],
        out_specs=list(qs),
        out_shape=[out, out],
        scratch_shapes=[pltpu.VMEM((2 * GDN_HG, dh, dh), F32)],
        compiler_params=_cparams("parallel", "parallel", "arbitrary"),
        name="gdn",
    )(qkv, qkv, qkv, qkv, qkv, qkv, dab, dab, da_row, da_row,
      a_log.reshape(2, DN_HEADS, 1, 1), dt_bias.reshape(2, DN_HEADS, 1, 1))


def _gdn_out_kernel(of_ref, ob_ref, z_ref, w_ref, o_ref):
    dh = DN_HEAD_DIM
    w = w_ref[...]
    for hh in range(DN_HEADS):
        cs = slice(hh * dh, (hh + 1) * dh)
        o = of_ref[0, :, cs] + ob_ref[0, :, cs]
        z = z_ref[0, :, cs].astype(F32)
        y = o * lax.rsqrt(jnp.mean(o * o, axis=1, keepdims=True) + RMS_EPS) * w
        o_ref[0, :, cs] = (y * (z * _sigmoid(z))).astype(o_ref.dtype)


def _gdn_out(o_f, o_b, dn, norm_w):
    bsz, s, cw = o_f.shape
    ts = _tile(s, 256)
    row = pl.BlockSpec((1, ts, cw), lambda b, i: (b, i, 0))
    return pl.pallas_call(
        _gdn_out_kernel,
        grid=(bsz, s // ts),
        in_specs=[row, row,
                  pl.BlockSpec((1, ts, cw), lambda b, i: (b, i, 3)),
                  pl.BlockSpec((1, DN_HEAD_DIM), lambda b, i: (0, 0))],
        out_specs=row,
        out_shape=jax.ShapeDtypeStruct((bsz, s, cw), BF16),
        compiler_params=_cparams("parallel", "parallel"),
        name="gdn_out",
    )(o_f, o_b, dn, norm_w.reshape(1, DN_HEAD_DIM))


def _merge_kernel(oa_ref, od_ref, wa_ref, wd_ref, ga_ref, gd_ref, o_ref):
    a = _dot(oa_ref[...], wa_ref[...])
    d = _dot(od_ref[...], wd_ref[...])
    o_ref[...] = (_sigmoid(ga_ref[...].astype(F32)) * a + _sigmoid(gd_ref[...].astype(F32)) * d).astype(o_ref.dtype)


def _merge(oa, od, wa, wd, gates):
    m, k = oa.shape
    n = wa.shape[1]
    tm, tn = _tile(m, 1024), _tile(n, 1024)
    nj = n // tn
    return pl.pallas_call(
        _merge_kernel,
        grid=(m // tm, nj),
        in_specs=[pl.BlockSpec((tm, k), lambda i, j: (i, 0)),
                  pl.BlockSpec((tm, k), lambda i, j: (i, 0)),
                  pl.BlockSpec((k, tn), lambda i, j: (0, j)),
                  pl.BlockSpec((k, tn), lambda i, j: (0, j)),
                  pl.BlockSpec((tm, tn), lambda i, j: (i, j)),
                  pl.BlockSpec((tm, tn), lambda i, j: (i, nj + j))],
        out_specs=pl.BlockSpec((tm, tn), lambda i, j: (i, j)),
        out_shape=jax.ShapeDtypeStruct((m, n), BF16),
        compiler_params=_cparams("parallel", "arbitrary"),
        name="branch_merge",
    )(oa, od, wa, wd, gates, gates)


def _topk_rows(x, payload=None):
    n = x.shape[0]
    rid = lax.broadcasted_iota(jnp.int32, x.shape, 0)
    vals, picks = [], []
    for _ in range(PEER_TOPK):
        mx = jnp.max(x, axis=0, keepdims=True)
        first = jnp.min(jnp.where(x == mx, rid, n), axis=0, keepdims=True)
        hit = rid == first
        vals.append(mx)
        picks.append(first if payload is None else jnp.max(jnp.where(hit, payload, -1), axis=0, keepdims=True))
        x = jnp.where(hit, -jnp.inf, x)
    return jnp.concatenate(vals, axis=0), jnp.concatenate(picks, axis=0)


def _route_kernel(q_ref, sk_ref, idx_ref, gate_ref):
    sv, si = [], []
    for p in range(2):
        sc = _dot_nt(sk_ref[0, p].astype(BF16), q_ref[:, p * N_KEYS:(p + 1) * N_KEYS].astype(BF16))
        v, r = _topk_rows(sc)
        sv.append(v)
        si.append(r)
    cand = jnp.concatenate([sv[0][k:k + 1] + sv[1] for k in range(PEER_TOPK)], axis=0)
    expert = jnp.concatenate([si[0][k:k + 1] * N_KEYS + si[1] for k in range(PEER_TOPK)], axis=0)
    fv, fe = _topk_rows(cand, expert)
    e = jnp.exp(fv - fv[0:1])
    idx_ref[0] = fe
    gate_ref[0] = e / jnp.sum(e, axis=0, keepdims=True)


def _peer_route(q, subkeys):
    t, qw = q.shape
    tt = _tile(t, 256)
    hq = qw // PEER_HEADS
    out = jax.ShapeDtypeStruct((PEER_HEADS, PEER_TOPK, t), jnp.int32)
    idx_t, gate_t = pl.pallas_call(
        _route_kernel,
        grid=(t // tt, PEER_HEADS),
        in_specs=[pl.BlockSpec((tt, hq), lambda i, h: (i, h)),
                  pl.BlockSpec((1, 2, N_KEYS, hq // 2), lambda i, h: (h, 0, 0, 0))],
        out_specs=[pl.BlockSpec((1, PEER_TOPK, tt), lambda i, h: (h, 0, i)),
                   pl.BlockSpec((1, PEER_TOPK, tt), lambda i, h: (h, 0, i))],
        out_shape=[out, jax.ShapeDtypeStruct(out.shape, F32)],
        compiler_params=_cparams("parallel", "parallel"),
        name="peer_route",
    )(q, subkeys)
    ne = PEER_HEADS * PEER_TOPK
    return idx_t.reshape(ne, t).T, gate_t.reshape(ne, t).T


def _gate_kernel(idx_ref, g_ref, o_ref, *, tt):
    ne = idx_ref.shape[1]
    sub = lax.broadcasted_iota(jnp.int32, (N_KEYS, ne), 0)

    def body(t, carry):
        idx = idx_ref[pl.ds(t, 1), :]
        g = g_ref[pl.ds(t, 1), :]
        i1 = lax.shift_right_logical(idx, 7)
        i2 = lax.bitwise_and(idx, N_KEYS - 1)
        a = jnp.where(sub == i1, g, 0.0).astype(BF16)
        bt = jnp.where(sub == i2, 1.0, 0.0).astype(BF16)
        o_ref[t] = _dot_nt(a, bt).astype(o_ref.dtype)
        return carry

    lax.fori_loop(0, tt, body, 0, unroll=8)


def _peer_gates(idx, gate):
    t, ne = idx.shape
    tt = _tile(t, 128)
    return pl.pallas_call(
        functools.partial(_gate_kernel, tt=tt),
        grid=(t // tt,),
        in_specs=[pl.BlockSpec((tt, ne), lambda i: (i, 0)),
                  pl.BlockSpec((tt, ne), lambda i: (i, 0))],
        out_specs=pl.BlockSpec((tt, N_KEYS, N_KEYS), lambda i: (i, 0, 0)),
        out_shape=jax.ShapeDtypeStruct((t, N_KEYS, N_KEYS), BF16),
        compiler_params=_cparams("parallel"),
        name="peer_gates",
    )(idx, gate)


def _peer_kernel(x_ref, u_ref, v_ref, g_ref, o_ref):
    n = pl.program_id(1)

    @pl.when(n == 0)
    def _():
        o_ref[...] = jnp.zeros(o_ref.shape, F32)

    hid = _dot_nt(x_ref[...], u_ref[...])
    act = 0.5 * hid * (1.0 + lax.erf(hid * (2.0 ** -0.5)))
    wgt = (g_ref[...].astype(F32) * act).astype(BF16)
    o_ref[...] += _dot(wgt, v_ref[...])


def _peer_dense(x, u, v, g):
    t, d = x.shape
    ne = u.shape[0]
    tm, tn = _tile(t, 512), _tile(ne, 512)
    return pl.pallas_call(
        _peer_kernel,
        grid=(t // tm, ne // tn),
        in_specs=[pl.BlockSpec((tm, d), lambda i, n: (i, 0)),
                  pl.BlockSpec((tn, d), lambda i, n: (n, 0)),
                  pl.BlockSpec((tn, d), lambda i, n: (n, 0)),
                  pl.BlockSpec((tm, tn), lambda i, n: (i, n))],
        out_specs=pl.BlockSpec((tm, d), lambda i, n: (i, 0)),
        out_shape=jax.ShapeDtypeStruct((t, d), F32),
        compiler_params=_cparams("parallel", "arbitrary"),
        name="peer_dense",
    )(x, u, v, g)


def _token_mixer(h, lam_init, w_in, lambda_qk, attn_subln_w, dn_conv_w, dn_a_log, dn_dt_bias, dn_norm_w,
                 w_br_attn, w_br_dn, w_out):
    bsz, s, d = h.shape
    t = bsz * s
    aw = ATTN_HEADS * 2 * ATTN_HEAD_DIM
    dw = DN_HEADS * DN_HEAD_DIM
    h2d = h.reshape(t, d)

    o0, o1, o3 = 3 * aw, 3 * aw + 4 * dw, 3 * aw + 4 * dw + 4 * DN_HEADS
    w_attn = w_in[:, :o0].astype(BF16)
    w_dn = w_in[:, o0:o1].astype(BF16)
    w_ab = jnp.zeros((d, LANES), BF16).at[:, :4 * DN_HEADS].set(w_in[:, o1:o3].astype(BF16))
    w_gate = w_in[:, o3:].astype(BF16)
    qkv = _matmul(h2d, w_attn, BF16, "proj_attn").reshape(bsz, s, 3 * aw)
    dn = _matmul(h2d, w_dn, BF16, "proj_dn").reshape(bsz, s, 4 * dw)
    dab = _matmul(h2d, w_ab, F32, "proj_ab").reshape(bsz, s, LANES)
    gates = _matmul(h2d, w_gate, BF16, "proj_gate")

    oa = _diff_attention(qkv, lambda_qk, attn_subln_w, lam_init)

    dqkv = _dn_conv(dn, dn_conv_w)
    da_row = jnp.transpose(dab[:, :, :2 * DN_HEADS], (0, 2, 1)).reshape(bsz, 2 * DN_HEADS, 1, s)
    o_f, o_b = _gdn(dqkv, dab, da_row, dn_a_log, dn_dt_bias)
    od = _gdn_out(o_f, o_b, dn, dn_norm_w)

    merged = _merge(oa.reshape(t, aw), od.reshape(t, dw), w_br_attn.astype(BF16), w_br_dn.astype(BF16), gates)
    return _matmul(merged, w_out.astype(BF16), F32, "out_proj").reshape(bsz, s, d)


def _peer_ffn(h, peer_wq, peer_subkeys, peer_u, peer_v):
    bsz, s, d = h.shape
    t = bsz * s
    h2d = h.reshape(t, d)
    pq = _matmul(h2d, peer_wq.astype(BF16), F32, "peer_q")
    idx, gate = _peer_route(pq, peer_subkeys)
    gmat = _peer_gates(idx, gate).reshape(t, N_KEYS * N_KEYS)
    return _peer_dense(h2d, peer_u.astype(BF16), peer_v.astype(BF16), gmat).reshape(bsz, s, d)


def kernel(x_prompt, x_sample, c_prompt, c_sample, w_ada, b_ada, w_in, lambda_qk, attn_subln_w, dn_conv_w, dn_a_log, dn_dt_bias, dn_norm_w, w_br_attn, w_br_dn, w_out, ln1_g, ln1_b, peer_wq, peer_subkeys, peer_u, peer_v, ln2_g, ln2_b):
    depth = w_ada.shape[0]
    alpha = (2.0 * depth) ** 0.25
    nb_prompt = x_prompt.shape[0]
    x = jnp.concatenate([x_prompt, x_sample], axis=0)
    c = jnp.concatenate([c_prompt, c_sample], axis=0)
    bsz, s, d = x.shape
    c8 = jnp.zeros((8, d), F32).at[:bsz].set(c)

    mods = []
    for l in range(depth):
        mod = _ada(c8, w_ada[l], b_ada[l])[:bsz]
        mods.append([mod[:, None, j * d:(j + 1) * d] for j in range(6)])

    h = _lnmod(x, mods[0][1], mods[0][0])
    for l in range(depth):
        sh1, sc1, g1, sh2, sc2, g2 = mods[l]
        lam_init = 0.8 - 0.6 * math.exp(-0.3 * l)
        y = _token_mixer(h, lam_init, w_in[l], lambda_qk[l], attn_subln_w[l], dn_conv_w[l], dn_a_log[l],
                         dn_dt_bias[l], dn_norm_w[l], w_br_attn[l], w_br_dn[l], w_out[l])
        x, h = _res_ln(x, y, g1, ln1_g[l], ln1_b[l], alpha, sc2, sh2)
        y = _peer_ffn(h, peer_wq[l], peer_subkeys[l], peer_u[l], peer_v[l])
        if l + 1 < depth:
            x, h = _res_ln(x, y, g2, ln2_g[l], ln2_b[l], alpha, mods[l + 1][1], mods[l + 1][0])
        else:
            x, _ = _res_ln(x, y, g2, ln2_g[l], ln2_b[l], alpha)

    return x[:nb_prompt], x[nb_prompt:]
```

```python
import functools
import math

import jax
import jax.numpy as jnp
from jax import lax
from jax.experimental import pallas as pl
from jax.experimental.pallas import tpu as pltpu

F32 = jnp.float32
BF16 = jnp.bfloat16

V7X_VMEM_BYTES = 64 * 1024 * 1024
VMEM_LIMIT = V7X_VMEM_BYTES - 6 * 1024 * 1024
LANES = 128

ATTN_HEADS = 8
ATTN_HEAD_DIM = 128
DN_HEADS = 16
DN_HEAD_DIM = 128
CONV_K = 5
CHUNK = 64
SUPER = 256
GDN_HG = 2
PEER_HEADS = 8
N_KEYS = 128
PEER_TOPK = 16
LN_EPS = 1e-5
RMS_EPS = 1e-6
LOG2E = 1.4426950408889634
NEG_BIG = -1e30


def _cparams(*sem):
    return pltpu.CompilerParams(dimension_semantics=sem, vmem_limit_bytes=VMEM_LIMIT)


def _tile(n, pref):
    t = min(n, pref)
    assert n % t == 0, (n, pref)
    return t


def _dot(a, b):
    return jnp.dot(a, b, preferred_element_type=F32)


def _dot_nt(a, b):
    return lax.dot_general(a, b, (((1,), (1,)), ((), ())), preferred_element_type=F32)


def _dot_tn(a, b):
    return lax.dot_general(a, b, (((0,), (0,)), ((), ())), preferred_element_type=F32)


def _sigmoid(x):
    return 1.0 / (1.0 + jnp.exp(-x))


def _layer_norm(x):
    mu = jnp.mean(x, axis=-1, keepdims=True)
    xc = x - mu
    var = jnp.mean(xc * xc, axis=-1, keepdims=True)
    return xc * lax.rsqrt(var + LN_EPS)


def _mm_kernel(a_ref, b_ref, o_ref):
    o_ref[...] = _dot(a_ref[...], b_ref[...]).astype(o_ref.dtype)


def _cast_kernel(x_ref, o_ref):
    o_ref[...] = x_ref[...].astype(o_ref.dtype)


def _to_bf16(w):
    r, c = w.shape
    tr = _tile(r, max(16, (1 << 20) // c // 16 * 16))
    return pl.pallas_call(
        _cast_kernel,
        grid=(r // tr,),
        in_specs=[pl.BlockSpec((tr, c), lambda i: (i, 0))],
        out_specs=pl.BlockSpec((tr, c), lambda i: (i, 0)),
        out_shape=jax.ShapeDtypeStruct((r, c), BF16),
        compiler_params=_cparams("parallel"),
        name="to_bf16",
    )(w)


def _matmul(a, b, out_dtype, name, tm=1024, tn=1024, n=None, col0=0):
    m, k = a.shape
    n = b.shape[1] if n is None else n
    tm, tn = _tile(m, tm), _tile(n, tn)
    assert col0 % tn == 0
    jo = col0 // tn
    return pl.pallas_call(
        _mm_kernel,
        grid=(m // tm, n // tn),
        in_specs=[pl.BlockSpec((tm, k), lambda i, j: (i, 0)),
                  pl.BlockSpec((k, tn), lambda i, j: (0, jo + j))],
        out_specs=pl.BlockSpec((tm, tn), lambda i, j: (i, j)),
        out_shape=jax.ShapeDtypeStruct((m, n), out_dtype),
        compiler_params=_cparams("parallel", "arbitrary"),
        name=name,
    )(a, b)


def _ada_kernel(c_ref, w_ref, b_ref, o_ref):
    c = c_ref[...]
    s = (c * _sigmoid(c)).astype(BF16)
    o_ref[...] = _dot(s, w_ref[...].astype(BF16)) + b_ref[...]


def _ada(c8, w, b):
    d, n = w.shape
    tn = _tile(n, 512)
    return pl.pallas_call(
        _ada_kernel,
        grid=(n // tn,),
        in_specs=[pl.BlockSpec((8, d), lambda j: (0, 0)),
                  pl.BlockSpec((d, tn), lambda j: (0, j)),
                  pl.BlockSpec((1, tn), lambda j: (0, j))],
        out_specs=pl.BlockSpec((8, tn), lambda j: (0, j)),
        out_shape=jax.ShapeDtypeStruct((8, n), F32),
        compiler_params=_cparams("arbitrary"),
        name="ada_mod",
    )(c8, w, b.reshape(1, n))


def _lnmod_kernel(x_ref, sc_ref, sh_ref, h_ref):
    h_ref[0] = (_layer_norm(x_ref[0]) * (1.0 + sc_ref[0]) + sh_ref[0]).astype(h_ref.dtype)


def _lnmod(x, sc, sh):
    bsz, s, d = x.shape
    tr = _tile(s, 256)
    row = pl.BlockSpec((1, tr, d), lambda b, i: (b, i, 0))
    vec = pl.BlockSpec((1, 1, d), lambda b, i: (b, 0, 0))
    return pl.pallas_call(
        _lnmod_kernel,
        grid=(bsz, s // tr),
        in_specs=[row, vec, vec],
        out_specs=row,
        out_shape=jax.ShapeDtypeStruct((bsz, s, d), BF16),
        compiler_params=_cparams("parallel", "parallel"),
        name="ln_mod",
    )(x, sc, sh)


def _res_ln_kernel(x_ref, y_ref, g_ref, lg_ref, lb_ref, *rest, alpha, with_h):
    z = alpha * x_ref[0] + g_ref[0] * y_ref[0].astype(F32)
    x1 = _layer_norm(z) * lg_ref[...] + lb_ref[...]
    if with_h:
        sc_ref, sh_ref, xo_ref, h_ref = rest
        h_ref[0] = (_layer_norm(x1) * (1.0 + sc_ref[0]) + sh_ref[0]).astype(h_ref.dtype)
    else:
        (xo_ref,) = rest
    xo_ref[0] = x1


def _res_ln(x, y, gate, ln_g, ln_b, alpha, sc=None, sh=None):
    bsz, s, d = x.shape
    tr = _tile(s, 256)
    row = pl.BlockSpec((1, tr, d), lambda b, i: (b, i, 0))
    vec = pl.BlockSpec((1, 1, d), lambda b, i: (b, 0, 0))
    par = pl.BlockSpec((1, d), lambda b, i: (0, 0))
    with_h = sc is not None
    ins = [x, y, gate, ln_g.reshape(1, d), ln_b.reshape(1, d)]
    in_specs = [row, row, vec, par, par]
    out_shape = [jax.ShapeDtypeStruct((bsz, s, d), F32)]
    out_specs = [row]
    if with_h:
        ins += [sc, sh]
        in_specs += [vec, vec]
        out_shape.append(jax.ShapeDtypeStruct((bsz, s, d), BF16))
        out_specs.append(row)
    out = pl.pallas_call(
        functools.partial(_res_ln_kernel, alpha=alpha, with_h=with_h),
        grid=(bsz, s // tr),
        in_specs=in_specs,
        out_specs=out_specs,
        out_shape=out_shape,
        compiler_params=_cparams("parallel", "parallel"),
        name="res_ln",
    )(*ins)
    return (out[0], out[1]) if with_h else (out[0], None)


ATTN_SKIP_LOG2 = 130.0
ATTN_REDO_LOG2 = 60.0


def _attn_kernel(lq_ref, w_ref, q_ref, k_ref, v_ref, o_ref, m_scr, l_scr, acc_scr, kn_scr, *, tq, tk, lam_init):
    dh = ATTN_HEAD_DIM
    h = pl.program_id(1)
    i = pl.program_id(2)
    s_len = k_ref.shape[1]
    nk = s_len // tk
    slope2 = jnp.exp2(-(jnp.full((1, 1), h + 1, jnp.int32).astype(F32))) * LOG2E
    c1 = dh ** -0.5 * LOG2E
    q0 = i * tq
    jd = lax.div(q0, tk)
    q = q_ref[0]
    qrel = lax.broadcasted_iota(jnp.int32, (tq, 1), 0).astype(F32)
    krel = lax.broadcasted_iota(jnp.int32, (1, tk), 1).astype(F32)

    @pl.when(i == 0)
    def _():
        def kn_body(j, mx):
            kk = k_ref[0, pl.ds(pl.multiple_of(j * tk, tk), tk), :].astype(F32)
            kk = kk * kk
            n2 = jnp.maximum(jnp.sum(kk[:, :dh], axis=1, keepdims=True), jnp.sum(kk[:, dh:], axis=1, keepdims=True))
            return jnp.maximum(mx, jnp.max(n2, axis=0, keepdims=True))
        kn2 = lax.fori_loop(0, nk, kn_body, jnp.zeros((1, 1), F32))
        kn_scr[...] = jnp.broadcast_to(kn2, kn_scr.shape)

    qf = q.astype(F32)
    kdg = k_ref[0, pl.ds(pl.multiple_of(q0, tq), tq), :].astype(F32)
    sii = [jnp.sum(qf[:, c * dh:(c + 1) * dh] * kdg[:, c * dh:(c + 1) * dh], axis=1, keepdims=True) * c1
           for c in range(2)]
    qq = qf * qf
    qn2 = jnp.max(jnp.maximum(jnp.sum(qq[:, :dh], axis=1, keepdims=True), jnp.sum(qq[:, dh:], axis=1, keepdims=True)),
                  axis=0, keepdims=True)
    sii_min = jnp.min(jnp.minimum(sii[0], sii[1]), axis=0, keepdims=True)
    bound = jnp.sqrt(qn2 * kn_scr[0:1, 0:1]) * c1 - sii_min
    reach = jnp.minimum((bound + ATTN_SKIP_LOG2) / (slope2 * tk), float(nk))
    nd = jnp.max(jnp.floor(reach).astype(jnp.int32)) + 1
    jlo = jnp.maximum(jd - nd, 0)
    jhi = jnp.minimum(jd + nd, nk - 1)

    def load_kv(j):
        start = pl.multiple_of(j * tk, tk)
        return k_ref[0, pl.ds(start, tk), :], v_ref[0, pl.ds(start, tk), :]

    def dist_bias(j):
        return slope2 * jnp.abs((qrel + (q0 - j * tk).astype(F32)) - krel)

    l_scr[...] = jnp.zeros(l_scr.shape, F32)
    acc_scr[...] = jnp.zeros(acc_scr.shape, F32)

    def accumulate(k, v, logit_fn):
        for c in range(2):
            s = _dot_nt(q[:, c * dh:(c + 1) * dh], k[:, c * dh:(c + 1) * dh])
            p = jnp.exp2(logit_fn(c, s))
            l_scr[c] += jnp.sum(p, axis=1, keepdims=True)
            acc_scr[c] += _dot(p.astype(BF16), v)

    row_left = [-slope2 * qrel - sii[c] for c in range(2)]
    row_right = [slope2 * qrel - sii[c] for c in range(2)]

    def left_body(j, carry):
        k, v = load_kv(j)
        col = slope2 * (krel - (q0 - j * tk).astype(F32))
        accumulate(k, v, lambda c, s: s * c1 + col + row_left[c])
        return carry

    def right_body(j, carry):
        k, v = load_kv(j)
        col = -slope2 * (krel + (j * tk - q0).astype(F32))
        accumulate(k, v, lambda c, s: s * c1 + col + row_right[c])
        return carry

    lax.fori_loop(jlo, jd, left_body, 0)
    k_diag, v_diag = load_kv(jd)
    bias_diag = dist_bias(jd)
    accumulate(k_diag, v_diag, lambda c, s: s * c1 - bias_diag - sii[c])
    lax.fori_loop(jd + 1, jhi + 1, right_body, 0)

    bad = jnp.maximum(jnp.max(jnp.where(l_scr[...] <= 2.0 ** ATTN_REDO_LOG2, 0.0, 1.0)),
                      jnp.max(jnp.where(jnp.isfinite(acc_scr[...]), 0.0, 1.0)))

    @pl.when(bad > 0.0)
    def _():
        m_scr[...] = jnp.full(m_scr.shape, NEG_BIG, F32)
        l_scr[...] = jnp.zeros(l_scr.shape, F32)
        acc_scr[...] = jnp.zeros(acc_scr.shape, F32)

        def body(j, carry):
            k, v = load_kv(j)
            bias = dist_bias(j)
            for c in range(2):
                s = _dot_nt(q[:, c * dh:(c + 1) * dh], k[:, c * dh:(c + 1) * dh]) * c1 - bias
                m_old = m_scr[c]
                m_new = jnp.maximum(m_old, jnp.max(s, axis=1, keepdims=True))
                alpha = jnp.exp2(m_old - m_new)
                p = jnp.exp2(s - m_new[:, :1])
                l_scr[c] = alpha * l_scr[c] + jnp.sum(p, axis=1, keepdims=True)
                acc_scr[c] = alpha[:, :1] * acc_scr[c] + _dot(p.astype(BF16), v)
                m_scr[c] = m_new
            return carry

        lax.fori_loop(jlo, jhi + 1, body, 0)

    lq = lq_ref[...]
    lam = (jnp.exp(jnp.sum(lq[0:1] * lq[1:2], axis=1, keepdims=True))
           - jnp.exp(jnp.sum(lq[2:3] * lq[3:4], axis=1, keepdims=True)) + lam_init)
    o = acc_scr[0] / l_scr[0][:, :1] - lam * (acc_scr[1] / l_scr[1][:, :1])
    ms = jnp.mean(o * o, axis=1, keepdims=True)
    o_ref[0] = (o * lax.rsqrt(ms + RMS_EPS) * w_ref[...] * (1.0 - lam_init)).astype(o_ref.dtype)


def _diff_attention(qkv, lambda_qk, subln_w, lam_init):
    bsz, s, _ = qkv.shape
    hw = 2 * ATTN_HEAD_DIM
    tq = _tile(s, 512)
    tk = _tile(s, 1024)
    assert tk % tq == 0
    return pl.pallas_call(
        functools.partial(_attn_kernel, tq=tq, tk=tk, lam_init=lam_init),
        grid=(bsz, ATTN_HEADS, s // tq),
        in_specs=[pl.BlockSpec((4, ATTN_HEAD_DIM), lambda b, h, i: (0, 0)),
                  pl.BlockSpec((1, hw), lambda b, h, i: (0, 0)),
                  pl.BlockSpec((1, tq, hw), lambda b, h, i: (b, i, h)),
                  pl.BlockSpec((1, s, hw), lambda b, h, i: (b, 0, ATTN_HEADS + h)),
                  pl.BlockSpec((1, s, hw), lambda b, h, i: (b, 0, 2 * ATTN_HEADS + h))],
        out_specs=pl.BlockSpec((1, tq, hw), lambda b, h, i: (b, i, h)),
        out_shape=jax.ShapeDtypeStruct((bsz, s, ATTN_HEADS * hw), BF16),
        scratch_shapes=[pltpu.VMEM((2, tq, LANES), F32),
                        pltpu.VMEM((2, tq, LANES), F32),
                        pltpu.VMEM((2, tq, hw), F32),
                        pltpu.VMEM((8, LANES), F32)],
        compiler_params=_cparams("parallel", "parallel", "arbitrary"),
        name="diff_attn",
    )(lambda_qk, subln_w.reshape(1, hw), qkv, qkv, qkv)


def _conv_kernel(prev_ref, cur_ref, next_ref, w_ref, o_ref, *, ts, halo):
    i = pl.program_id(1)
    p = pl.program_id(2)
    ns = pl.num_programs(1)
    dh = DN_HEAD_DIM
    x = cur_ref[0].astype(F32)
    pv = jnp.where(i > 0, prev_ref[0].astype(F32), 0.0)
    nx = jnp.where(i < ns - 1, next_ref[0].astype(F32), 0.0)
    w = w_ref[...]
    row = lax.broadcasted_iota(jnp.int32, (ts, 1), 0)
    xm1 = jnp.where(row == 0, pv[halo - 1:halo], pltpu.roll(x, 1, 0))
    xm2 = jnp.where(row == 0, pv[halo - 2:halo - 1], jnp.where(row == 1, pv[halo - 1:halo], pltpu.roll(x, 2, 0)))
    xp1 = jnp.where(row == ts - 1, nx[0:1], pltpu.roll(x, ts - 1, 0))
    xp2 = jnp.where(row == ts - 2, nx[0:1], jnp.where(row == ts - 1, nx[1:2], pltpu.roll(x, ts - 2, 0)))
    y = w[0:1] * xm2 + w[1:2] * xm1 + w[2:3] * x + w[3:4] * xp1 + w[4:5] * xp2
    y = y * _sigmoid(y)
    qscale = jnp.where(p == 0, dh ** -0.5, 1.0)
    for hh in range(y.shape[1] // dh):
        seg = y[:, hh * dh:(hh + 1) * dh]
        nrm = lax.rsqrt(jnp.sum(seg * seg, axis=1, keepdims=True) + RMS_EPS) * qscale
        fac = jnp.where(p == 2, 1.0, nrm)
        o_ref[0, :, hh * dh:(hh + 1) * dh] = (seg * fac).astype(o_ref.dtype)


def _dn_conv(dn, conv_w):
    bsz, s, _ = dn.shape
    cw = DN_HEADS * DN_HEAD_DIM
    ts = _tile(s, 256)
    halo = 16
    hb = ts // halo
    nhb = s // halo
    return pl.pallas_call(
        functools.partial(_conv_kernel, ts=ts, halo=halo),
        grid=(bsz, s // ts, 3),
        in_specs=[pl.BlockSpec((1, halo, cw), lambda b, i, p: (b, jnp.maximum(i * hb - 1, 0), p)),
                  pl.BlockSpec((1, ts, cw), lambda b, i, p: (b, i, p)),
                  pl.BlockSpec((1, halo, cw), lambda b, i, p: (b, jnp.minimum((i + 1) * hb, nhb - 1), p)),
                  pl.BlockSpec((CONV_K, cw), lambda b, i, p: (0, p))],
        out_specs=pl.BlockSpec((1, ts, cw), lambda b, i, p: (b, i, p)),
        out_shape=jax.ShapeDtypeStruct((bsz, s, 3 * cw), BF16),
        compiler_params=_cparams("parallel", "parallel", "parallel"),
        name="dn_conv",
    )(dn, dn, dn, conv_w)


def _split3(x):
    hi = x.astype(BF16)
    r1 = x - hi.astype(F32)
    mid = r1.astype(BF16)
    lo = (r1 - mid.astype(F32)).astype(BF16)
    return hi, mid, lo


def _softplus(x):
    return jnp.maximum(x, 0.0) + jnp.log1p(jnp.exp(-jnp.abs(x)))


def _gdn_masks(rev):
    r = lax.broadcasted_iota(jnp.int32, (SUPER, SUPER), 0)
    c = lax.broadcasted_iota(jnp.int32, (SUPER, SUPER), 1)
    same = (r // CHUNK) == (c // CHUNK)
    incl = jnp.logical_and(same, (c >= r) if rev else (c <= r))
    strict = jnp.logical_and(same, (c > r) if rev else (c < r))
    return dict(
        incl=incl, strict=strict,
        incl01=jnp.where(incl, 1.0, 0.0).astype(BF16),
        eye=jnp.where(r == c, 1.0, 0.0),
        lvl=31 - lax.clz(lax.bitwise_xor(r, c)))


def _gdn_superchunks(ps):
    dh = DN_HEAD_DIM
    n = range(len(ps))
    mk = [p["mk"] for p in ps]
    kf = [p["k"].astype(F32) for p in ps]
    g_col = [jnp.broadcast_to(p["neg_a"] * _softplus(p["da_col"] + p["dtb"]), (SUPER, dh)) for p in ps]
    beta = [jnp.broadcast_to(_sigmoid(p["db_col"]), (SUPER, dh)) for p in ps]
    gc = [sum(_dot(mk[i]["incl01"], t) for t in _split3(g_col[i])) for i in n]
    tot = [jnp.concatenate([jnp.broadcast_to(gc[i][r:r + 1], (CHUNK, dh))
                            for r in range(0 if ps[i]["rev"] else CHUNK - 1, SUPER, CHUNK)], axis=0) for i in n]
    gc_row = [jnp.transpose(gc[i])[0:1] for i in n]
    eg = [jnp.exp(g) for g in gc]
    ekd = [jnp.exp(tot[i] - gc[i]) for i in n]
    decay = [jnp.exp(jnp.where(mk[i]["incl"], jnp.concatenate([gc[i], gc[i]], axis=1) - gc_row[i], NEG_BIG))
             for i in n]
    kb = [kf[i] * beta[i] for i in n]
    vb = [ps[i]["v"].astype(F32) * beta[i] for i in n]
    mm = [jnp.where(mk[i]["strict"], _dot_nt(kb[i].astype(BF16), ps[i]["k"]) * decay[i], 0.0) for i in n]
    t_inv = [mk[i]["eye"] - jnp.where(mk[i]["lvl"] == 0, mm[i], 0.0) for i in n]
    for j in range(1, 6):
        tb = [t.astype(BF16) for t in t_inv]
        wj = [_dot(tb[i], jnp.where(mk[i]["lvl"] == j, mm[i], 0.0).astype(BF16)) for i in n]
        t_inv = [t_inv[i] - _dot(wj[i].astype(BF16), tb[i]) for i in n]
    rhs = [jnp.concatenate([kb[i] * eg[i], vb[i]], axis=1).astype(BF16) for i in n]
    wub = [_dot(t_inv[i].astype(BF16), rhs[i]).astype(BF16) for i in n]
    qk = [jnp.where(mk[i]["incl"], _dot_nt(ps[i]["q"], ps[i]["k"]) * decay[i], 0.0) for i in n]
    ab = [_dot(qk[i].astype(BF16), wub[i]) for i in n]
    aq = [(ps[i]["q"].astype(F32) * eg[i] - ab[i][:, :dh]).astype(BF16) for i in n]
    kd = [(kf[i] * ekd[i]).astype(BF16) for i in n]
    nch = SUPER // CHUNK
    for step in range(nch):
        for i in n:
            p = ps[i]
            ch = nch - 1 - step if p["rev"] else step
            cr = slice(ch * CHUNK, (ch + 1) * CHUNK)
            st = p["s"][...]
            stb = st.astype(BF16)
            orow = slice(p["row0"] + ch * CHUNK, p["row0"] + (ch + 1) * CHUNK)
            p["o"][0, orow, p["col0"]:p["col0"] + dh] = _dot(aq[i][cr], stb) + ab[i][cr, dh:]
            pq = _dot_tn(kd[i][cr], wub[i][cr])
            egl = jnp.exp(tot[i][ch * CHUNK:ch * CHUNK + 1, :])
            p["s"][...] = egl * st - _dot(pq[:, :dh].astype(BF16), stb) + pq[:, dh:]


def _gdn_kernel(qf_ref, kf_ref, vf_ref, qb_ref, kb_ref, vb_ref, cf_ref, cb_ref, al_ref, dtb_ref,
                of_ref, ob_ref, s_scr, *, ts):
    dh = DN_HEAD_DIM
    hg = pl.program_id(1)
    nsc = ts // SUPER

    @pl.when(pl.program_id(2) == 0)
    def _():
        s_scr[...] = jnp.zeros(s_scr.shape, F32)

    lane = lax.broadcasted_iota(jnp.int32, (1, LANES), 1)
    chains = []
    for d, (q_ref, k_ref, v_ref, c_ref, o_ref) in enumerate(
            ((qf_ref, kf_ref, vf_ref, cf_ref, of_ref), (qb_ref, kb_ref, vb_ref, cb_ref, ob_ref))):
        mk = _gdn_masks(bool(d))
        for hh in range(GDN_HG):
            h = hg * GDN_HG + hh
            chains.append(dict(q=q_ref, k=k_ref, v=v_ref, c=c_ref, o=o_ref, mk=mk, d=d, hh=hh,
                               neg_a=-jnp.exp(al_ref[d, hh]), dtb=dtb_ref[d, hh],
                               asel=lane == d * DN_HEADS + h, bsel=lane == (2 + d) * DN_HEADS + h))

    for step in range(nsc):
        ps = []
        for cc in chains:
            d, hh = cc["d"], cc["hh"]
            sc = nsc - 1 - step if d else step
            rows = slice(sc * SUPER, (sc + 1) * SUPER)
            hc = slice(hh * dh, (hh + 1) * dh)
            cols = cc["c"][0, rows, :]
            ps.append(dict(q=cc["q"][0, rows, hc], k=cc["k"][0, rows, hc], v=cc["v"][0, rows, hc],
                           da_col=jnp.sum(jnp.where(cc["asel"], cols, 0.0), axis=1, keepdims=True),
                           db_col=jnp.sum(jnp.where(cc["bsel"], cols, 0.0), axis=1, keepdims=True),
                           neg_a=cc["neg_a"], dtb=cc["dtb"], mk=cc["mk"],
                           s=s_scr.at[d * GDN_HG + hh], o=cc["o"], row0=sc * SUPER, col0=hh * dh, rev=bool(d)))
        _gdn_superchunks(ps)


def _gdn(qkv, dab, a_log, dt_bias):
    bsz, s, _ = qkv.shape
    dh = DN_HEAD_DIM
    ts = _tile(s, 1024)
    n = s // ts

    def spec(shape, fn):
        return (pl.BlockSpec(shape, lambda b, h, i: fn(b, h, i)),
                pl.BlockSpec(shape, lambda b, h, i: fn(b, h, n - 1 - i)))

    ng = DN_HEADS // GDN_HG
    gw = GDN_HG * dh
    qs = spec((1, ts, gw), lambda b, h, t: (b, t, h))
    ks = spec((1, ts, gw), lambda b, h, t: (b, t, ng + h))
    vs = spec((1, ts, gw), lambda b, h, t: (b, t, 2 * ng + h))
    cs = spec((1, ts, LANES), lambda b, h, t: (b, t, 0))
    par = pl.BlockSpec((2, GDN_HG, 1, 1), lambda b, h, i: (0, h, 0, 0))
    out = jax.ShapeDtypeStruct((bsz, s, DN_HEADS * dh), F32)
    return pl.pallas_call(
        functools.partial(_gdn_kernel, ts=ts),
        grid=(bsz, ng, n),
        in_specs=[qs[0], ks[0], vs[0], qs[1], ks[1], vs[1], cs[0], cs[1], par, par],
        out_specs=list(qs),
        out_shape=[out, out],
        scratch_shapes=[pltpu.VMEM((2 * GDN_HG, dh, dh), F32)],
        compiler_params=_cparams("parallel", "parallel", "arbitrary"),
        name="gdn",
    )(qkv, qkv, qkv, qkv, qkv, qkv, dab, dab,
      a_log.reshape(2, DN_HEADS, 1, 1), dt_bias.reshape(2, DN_HEADS, 1, 1))


def _gdn_out_kernel(of_ref, ob_ref, z_ref, w_ref, o_ref):
    dh = DN_HEAD_DIM
    w = w_ref[...]
    for hh in range(DN_HEADS):
        cs = slice(hh * dh, (hh + 1) * dh)
        o = of_ref[0, :, cs] + ob_ref[0, :, cs]
        z = z_ref[0, :, cs].astype(F32)
        y = o * lax.rsqrt(jnp.mean(o * o, axis=1, keepdims=True) + RMS_EPS) * w
        o_ref[0, :, cs] = (y * (z * _sigmoid(z))).astype(o_ref.dtype)


def _gdn_out(o_f, o_b, dn, norm_w):
    bsz, s, cw = o_f.shape
    ts = _tile(s, 256)
    row = pl.BlockSpec((1, ts, cw), lambda b, i: (b, i, 0))
    return pl.pallas_call(
        _gdn_out_kernel,
        grid=(bsz, s // ts),
        in_specs=[row, row,
                  pl.BlockSpec((1, ts, cw), lambda b, i: (b, i, 3)),
                  pl.BlockSpec((1, DN_HEAD_DIM), lambda b, i: (0, 0))],
        out_specs=row,
        out_shape=jax.ShapeDtypeStruct((bsz, s, cw), BF16),
        compiler_params=_cparams("parallel", "parallel"),
        name="gdn_out",
    )(o_f, o_b, dn, norm_w.reshape(1, DN_HEAD_DIM))


def _merge_kernel(oa_ref, od_ref, wa_ref, wd_ref, ga_ref, gd_ref, o_ref):
    a = _dot(oa_ref[...], wa_ref[...])
    d = _dot(od_ref[...], wd_ref[...])
    o_ref[...] = (_sigmoid(ga_ref[...].astype(F32)) * a + _sigmoid(gd_ref[...].astype(F32)) * d).astype(o_ref.dtype)


def _merge(oa, od, wa, wd, gates):
    m, k = oa.shape
    n = wa.shape[1]
    tm, tn = _tile(m, 1024), _tile(n, 1024)
    nj = n // tn
    return pl.pallas_call(
        _merge_kernel,
        grid=(m // tm, nj),
        in_specs=[pl.BlockSpec((tm, k), lambda i, j: (i, 0)),
                  pl.BlockSpec((tm, k), lambda i, j: (i, 0)),
                  pl.BlockSpec((k, tn), lambda i, j: (0, j)),
                  pl.BlockSpec((k, tn), lambda i, j: (0, j)),
                  pl.BlockSpec((tm, tn), lambda i, j: (i, j)),
                  pl.BlockSpec((tm, tn), lambda i, j: (i, nj + j))],
        out_specs=pl.BlockSpec((tm, tn), lambda i, j: (i, j)),
        out_shape=jax.ShapeDtypeStruct((m, n), BF16),
        compiler_params=_cparams("parallel", "arbitrary"),
        name="branch_merge",
    )(oa, od, wa, wd, gates, gates)


def _topk_rows(x, payload=None):
    n = x.shape[0]
    rid = lax.broadcasted_iota(jnp.int32, x.shape, 0)
    vals, picks = [], []
    for _ in range(PEER_TOPK):
        mx = jnp.max(x, axis=0, keepdims=True)
        first = jnp.min(jnp.where(x == mx, rid, n), axis=0, keepdims=True)
        hit = rid == first
        vals.append(mx)
        picks.append(first if payload is None else jnp.max(jnp.where(hit, payload, -1), axis=0, keepdims=True))
        x = jnp.where(hit, -jnp.inf, x)
    return jnp.concatenate(vals, axis=0), jnp.concatenate(picks, axis=0)


def _route_kernel(q_ref, sk_ref, idx_ref, gate_ref):
    sv, si = [], []
    for p in range(2):
        sc = _dot_nt(sk_ref[0, p].astype(BF16), q_ref[:, p * N_KEYS:(p + 1) * N_KEYS].astype(BF16))
        v, r = _topk_rows(sc)
        sv.append(v)
        si.append(r)
    half = PEER_TOPK // 2

    def pairs(a, b, op):
        return jnp.concatenate([op(a[0:1], b)] + [op(a[k:k + 1], b[0:half]) for k in range(1, half)]
                               + [op(a[half:], b[0:1])], axis=0)

    cand = pairs(sv[0], sv[1], lambda x, y: x + y)
    expert = pairs(si[0], si[1], lambda x, y: x * N_KEYS + y)
    fv, fe = _topk_rows(cand, expert)
    e = jnp.exp(fv - fv[0:1])
    idx_ref[0] = fe
    gate_ref[0] = e / jnp.sum(e, axis=0, keepdims=True)


def _peer_route(q, subkeys):
    t, qw = q.shape
    tt = _tile(t, 256)
    hq = qw // PEER_HEADS
    out = jax.ShapeDtypeStruct((PEER_HEADS, PEER_TOPK, t), jnp.int32)
    idx_t, gate_t = pl.pallas_call(
        _route_kernel,
        grid=(t // tt, PEER_HEADS),
        in_specs=[pl.BlockSpec((tt, hq), lambda i, h: (i, h)),
                  pl.BlockSpec((1, 2, N_KEYS, hq // 2), lambda i, h: (h, 0, 0, 0))],
        out_specs=[pl.BlockSpec((1, PEER_TOPK, tt), lambda i, h: (h, 0, i)),
                   pl.BlockSpec((1, PEER_TOPK, tt), lambda i, h: (h, 0, i))],
        out_shape=[out, jax.ShapeDtypeStruct(out.shape, F32)],
        compiler_params=_cparams("parallel", "parallel"),
        name="peer_route",
    )(q, subkeys)
    ne = PEER_HEADS * PEER_TOPK
    return idx_t.reshape(ne, t).T, gate_t.reshape(ne, t).T


def _gate_kernel(idx_ref, g_ref, o_ref, *, tt):
    ne = idx_ref.shape[1]
    sub = lax.broadcasted_iota(jnp.int32, (N_KEYS, ne), 0)

    def body(t, carry):
        idx = idx_ref[pl.ds(t, 1), :]
        g = g_ref[pl.ds(t, 1), :]
        i1 = lax.shift_right_logical(idx, 7)
        i2 = lax.bitwise_and(idx, N_KEYS - 1)
        a = jnp.where(sub == i1, g, 0.0).astype(BF16)
        bt = jnp.where(sub == i2, 1.0, 0.0).astype(BF16)
        o_ref[t] = _dot_nt(a, bt).astype(o_ref.dtype)
        return carry

    lax.fori_loop(0, tt, body, 0, unroll=8)


def _peer_gates(idx, gate):
    t, ne = idx.shape
    tt = _tile(t, 128)
    return pl.pallas_call(
        functools.partial(_gate_kernel, tt=tt),
        grid=(t // tt,),
        in_specs=[pl.BlockSpec((tt, ne), lambda i: (i, 0)),
                  pl.BlockSpec((tt, ne), lambda i: (i, 0))],
        out_specs=pl.BlockSpec((tt, N_KEYS, N_KEYS), lambda i: (i, 0, 0)),
        out_shape=jax.ShapeDtypeStruct((t, N_KEYS, N_KEYS), BF16),
        compiler_params=_cparams("parallel"),
        name="peer_gates",
    )(idx, gate)


def _peer_kernel(x_ref, u_ref, v_ref, g_ref, o_ref):
    n = pl.program_id(1)

    @pl.when(n == 0)
    def _():
        o_ref[...] = jnp.zeros(o_ref.shape, F32)

    hid = _dot_nt(x_ref[...], u_ref[...])
    act = 0.5 * hid * (1.0 + lax.erf(hid * (2.0 ** -0.5)))
    wgt = (g_ref[...].astype(F32) * act).astype(BF16)
    o_ref[...] += _dot(wgt, v_ref[...])


def _peer_dense(x, u, v, g):
    t, d = x.shape
    ne = u.shape[0]
    tm = next(c for c in (384, 256, 128, t) if t % c == 0)
    tn = _tile(ne, 1024)
    return pl.pallas_call(
        _peer_kernel,
        grid=(t // tm, ne // tn),
        in_specs=[pl.BlockSpec((tm, d), lambda i, n: (i, 0)),
                  pl.BlockSpec((tn, d), lambda i, n: (n, 0)),
                  pl.BlockSpec((tn, d), lambda i, n: (n, 0)),
                  pl.BlockSpec((tm, tn), lambda i, n: (i, n))],
        out_specs=pl.BlockSpec((tm, d), lambda i, n: (i, 0)),
        out_shape=jax.ShapeDtypeStruct((t, d), F32),
        compiler_params=_cparams("parallel", "arbitrary"),
        name="peer_dense",
    )(x, u, v, g)


def _token_mixer(h, lam_init, w_in, lambda_qk, attn_subln_w, dn_conv_w, dn_a_log, dn_dt_bias, dn_norm_w,
                 w_br_attn, w_br_dn, w_out):
    bsz, s, d = h.shape
    t = bsz * s
    aw = ATTN_HEADS * 2 * ATTN_HEAD_DIM
    dw = DN_HEADS * DN_HEAD_DIM
    h2d = h.reshape(t, d)

    o0, o1, o3 = 3 * aw, 3 * aw + 4 * dw, 3 * aw + 4 * dw + 4 * DN_HEADS
    wb = _to_bf16(w_in)
    qkv = _matmul(h2d, wb, BF16, "proj_attn", n=o0).reshape(bsz, s, 3 * aw)
    dn = _matmul(h2d, wb, BF16, "proj_dn", n=o1 - o0, col0=o0).reshape(bsz, s, 4 * dw)
    dab = _matmul(h2d, wb, F32, "proj_ab", n=LANES, col0=o1).reshape(bsz, s, LANES)
    gates = _matmul(h2d, wb[:, o3:], BF16, "proj_gate")

    oa = _diff_attention(qkv, lambda_qk, attn_subln_w, lam_init)

    dqkv = _dn_conv(dn, dn_conv_w)
    o_f, o_b = _gdn(dqkv, dab, dn_a_log, dn_dt_bias)
    od = _gdn_out(o_f, o_b, dn, dn_norm_w)

    merged = _merge(oa.reshape(t, aw), od.reshape(t, dw), _to_bf16(w_br_attn), _to_bf16(w_br_dn), gates)
    return _matmul(merged, _to_bf16(w_out), F32, "out_proj").reshape(bsz, s, d)


def _peer_ffn(h, peer_wq, peer_subkeys, peer_u, peer_v):
    bsz, s, d = h.shape
    t = bsz * s
    h2d = h.reshape(t, d)
    pq = _matmul(h2d, _to_bf16(peer_wq), F32, "peer_q")
    idx, gate = _peer_route(pq, peer_subkeys)
    gmat = _peer_gates(idx, gate).reshape(t, N_KEYS * N_KEYS)
    return _peer_dense(h2d, _to_bf16(peer_u), _to_bf16(peer_v), gmat).reshape(bsz, s, d)


def kernel(x_prompt, x_sample, c_prompt, c_sample, w_ada, b_ada, w_in, lambda_qk, attn_subln_w, dn_conv_w, dn_a_log, dn_dt_bias, dn_norm_w, w_br_attn, w_br_dn, w_out, ln1_g, ln1_b, peer_wq, peer_subkeys, peer_u, peer_v, ln2_g, ln2_b):
    depth = w_ada.shape[0]
    alpha = (2.0 * depth) ** 0.25
    nb_prompt = x_prompt.shape[0]
    x = jnp.concatenate([x_prompt, x_sample], axis=0)
    c = jnp.concatenate([c_prompt, c_sample], axis=0)
    bsz, s, d = x.shape
    c8 = jnp.zeros((8, d), F32).at[:bsz].set(c)

    mods = []
    for l in range(depth):
        mod = _ada(c8, w_ada[l], b_ada[l])[:bsz]
        mods.append([mod[:, None, j * d:(j + 1) * d] for j in range(6)])

    h = _lnmod(x, mods[0][1], mods[0][0])
    for l in range(depth):
        sh1, sc1, g1, sh2, sc2, g2 = mods[l]
        lam_init = 0.8 - 0.6 * math.exp(-0.3 * l)
        y = _token_mixer(h, lam_init, w_in[l], lambda_qk[l], attn_subln_w[l], dn_conv_w[l], dn_a_log[l],
                         dn_dt_bias[l], dn_norm_w[l], w_br_attn[l], w_br_dn[l], w_out[l])
        x, h = _res_ln(x, y, g1, ln1_g[l], ln1_b[l], alpha, sc2, sh2)
        y = _peer_ffn(h, peer_wq[l], peer_subkeys[l], peer_u[l], peer_v[l])
        if l + 1 < depth:
            x, h = _res_ln(x, y, g2, ln2_g[l], ln2_b[l], alpha, mods[l + 1][1], mods[l + 1][0])
        else:
            x, _ = _res_ln(x, y, g2, ln2_g[l], ln2_b[l], alpha)

    return x[:nb_prompt], x[nb_prompt:]
```

```python
import functools
import math

import jax
import jax.numpy as jnp
from jax import lax
from jax.experimental import pallas as pl
from jax.experimental.pallas import tpu as pltpu

F32 = jnp.float32
BF16 = jnp.bfloat16

V7X_VMEM_BYTES = 64 * 1024 * 1024
VMEM_LIMIT = V7X_VMEM_BYTES - 6 * 1024 * 1024
LANES = 128

ATTN_HEADS = 8
ATTN_HEAD_DIM = 128
DN_HEADS = 16
DN_HEAD_DIM = 128
CONV_K = 5
CHUNK = 64
SUPER = 256
GDN_HG = 2
PEER_HEADS = 8
N_KEYS = 128
PEER_TOPK = 16
LN_EPS = 1e-5
RMS_EPS = 1e-6
LOG2E = 1.4426950408889634
NEG_BIG = -1e30


def _cparams(*sem):
    return pltpu.CompilerParams(dimension_semantics=sem, vmem_limit_bytes=VMEM_LIMIT)


def _tile(n, pref):
    t = min(n, pref)
    assert n % t == 0, (n, pref)
    return t


def _dot(a, b):
    return jnp.dot(a, b, preferred_element_type=F32)


def _dot_nt(a, b):
    return lax.dot_general(a, b, (((1,), (1,)), ((), ())), preferred_element_type=F32)


def _dot_tn(a, b):
    return lax.dot_general(a, b, (((0,), (0,)), ((), ())), preferred_element_type=F32)


def _sigmoid(x):
    return 1.0 / (1.0 + jnp.exp(-x))


def _layer_norm(x):
    mu = jnp.mean(x, axis=-1, keepdims=True)
    xc = x - mu
    var = jnp.mean(xc * xc, axis=-1, keepdims=True)
    return xc * lax.rsqrt(var + LN_EPS)


def _mm_kernel(a_ref, b_ref, o_ref):
    o_ref[...] = _dot(a_ref[...], b_ref[...]).astype(o_ref.dtype)


def _cast_kernel(x_ref, o_ref):
    o_ref[...] = x_ref[...].astype(o_ref.dtype)


def _to_bf16(w, l):
    _, r, c = w.shape
    tr = _tile(r, max(16, (1 << 20) // c // 16 * 16))
    return pl.pallas_call(
        _cast_kernel,
        grid=(r // tr,),
        in_specs=[pl.BlockSpec((None, tr, c), lambda i: (l, i, 0))],
        out_specs=pl.BlockSpec((tr, c), lambda i: (i, 0)),
        out_shape=jax.ShapeDtypeStruct((r, c), BF16),
        compiler_params=_cparams("parallel"),
        name="to_bf16",
    )(w)


def _matmul(a, b, out_dtype, name, tm=1024, tn=1024, n=None, col0=0):
    m, k = a.shape
    n = b.shape[1] if n is None else n
    tm, tn = _tile(m, tm), _tile(n, tn)
    assert col0 % tn == 0
    jo = col0 // tn
    return pl.pallas_call(
        _mm_kernel,
        grid=(m // tm, n // tn),
        in_specs=[pl.BlockSpec((tm, k), lambda i, j: (i, 0)),
                  pl.BlockSpec((k, tn), lambda i, j: (0, jo + j))],
        out_specs=pl.BlockSpec((tm, tn), lambda i, j: (i, j)),
        out_shape=jax.ShapeDtypeStruct((m, n), out_dtype),
        compiler_params=_cparams("parallel", "arbitrary"),
        name=name,
    )(a, b)


def _ada_kernel(c_ref, w_ref, b_ref, o_ref):
    c = c_ref[...]
    s = (c * _sigmoid(c)).astype(BF16)
    o_ref[...] = _dot(s, w_ref[...].astype(BF16)) + b_ref[...]


def _ada(c8, w, b, l):
    depth, d, n = w.shape
    tn = _tile(n, 512)
    return pl.pallas_call(
        _ada_kernel,
        grid=(n // tn,),
        in_specs=[pl.BlockSpec((8, d), lambda j: (0, 0)),
                  pl.BlockSpec((None, d, tn), lambda j: (l, 0, j)),
                  pl.BlockSpec((None, 1, tn), lambda j: (l, 0, j))],
        out_specs=pl.BlockSpec((8, tn), lambda j: (0, j)),
        out_shape=jax.ShapeDtypeStruct((8, n), F32),
        compiler_params=_cparams("arbitrary"),
        name="ada_mod",
    )(c8, w, b.reshape(depth, 1, n))


def _lnmod_kernel(x_ref, sc_ref, sh_ref, h_ref):
    h_ref[0] = (_layer_norm(x_ref[0]) * (1.0 + sc_ref[0]) + sh_ref[0]).astype(h_ref.dtype)


def _lnmod(x, sc, sh):
    bsz, s, d = x.shape
    tr = _tile(s, 256)
    row = pl.BlockSpec((1, tr, d), lambda b, i: (b, i, 0))
    vec = pl.BlockSpec((1, 1, d), lambda b, i: (b, 0, 0))
    return pl.pallas_call(
        _lnmod_kernel,
        grid=(bsz, s // tr),
        in_specs=[row, vec, vec],
        out_specs=row,
        out_shape=jax.ShapeDtypeStruct((bsz, s, d), BF16),
        compiler_params=_cparams("parallel", "parallel"),
        name="ln_mod",
    )(x, sc, sh)


def _res_ln_kernel(x_ref, y_ref, g_ref, lg_ref, lb_ref, *rest, alpha, with_h):
    z = alpha * x_ref[0] + g_ref[0] * y_ref[0].astype(F32)
    x1 = _layer_norm(z) * lg_ref[...] + lb_ref[...]
    if with_h:
        sc_ref, sh_ref, xo_ref, h_ref = rest
        h_ref[0] = (_layer_norm(x1) * (1.0 + sc_ref[0]) + sh_ref[0]).astype(h_ref.dtype)
    else:
        (xo_ref,) = rest
    xo_ref[0] = x1


def _res_ln(x, y, gate, ln_g, ln_b, alpha, sc=None, sh=None, b0=0, nb=None):
    bsz, s, d = x.shape
    nb = bsz - b0 if nb is None else nb
    tr = _tile(s, 256)
    row_in = pl.BlockSpec((1, tr, d), lambda b, i: (b0 + b, i, 0))
    row_out = pl.BlockSpec((1, tr, d), lambda b, i: (b, i, 0))
    vec = pl.BlockSpec((1, 1, d), lambda b, i: (b0 + b, 0, 0))
    par = pl.BlockSpec((1, d), lambda b, i: (0, 0))
    with_h = sc is not None
    ins = [x, y, gate, ln_g.reshape(1, d), ln_b.reshape(1, d)]
    in_specs = [row_in, row_in, vec, par, par]
    out_shape = [jax.ShapeDtypeStruct((nb, s, d), F32)]
    out_specs = [row_out]
    if with_h:
        ins += [sc, sh]
        in_specs += [vec, vec]
        out_shape.append(jax.ShapeDtypeStruct((nb, s, d), BF16))
        out_specs.append(row_out)
    out = pl.pallas_call(
        functools.partial(_res_ln_kernel, alpha=alpha, with_h=with_h),
        grid=(nb, s // tr),
        in_specs=in_specs,
        out_specs=out_specs,
        out_shape=out_shape,
        compiler_params=_cparams("parallel", "parallel"),
        name="res_ln",
    )(*ins)
    return (out[0], out[1]) if with_h else (out[0], None)


ATTN_SKIP_LOG2 = 130.0
ATTN_REDO_LOG2 = 60.0


def _attn_kernel(lq_ref, w_ref, q_ref, k_ref, v_ref, o_ref, m_scr, l_scr, acc_scr, kn_scr, *, tq, tk, lam_init):
    dh = ATTN_HEAD_DIM
    h = pl.program_id(1)
    i = pl.program_id(2)
    s_len = k_ref.shape[1]
    nk = s_len // tk
    slope2 = jnp.exp2(-(jnp.full((1, 1), h + 1, jnp.int32).astype(F32))) * LOG2E
    c1 = dh ** -0.5 * LOG2E
    q0 = i * tq
    jd = lax.div(q0, tk)
    q = q_ref[0]
    qrel = lax.broadcasted_iota(jnp.int32, (tq, 1), 0).astype(F32)
    krel = lax.broadcasted_iota(jnp.int32, (1, tk), 1).astype(F32)

    @pl.when(i == 0)
    def _():
        def kn_body(j, mx):
            kk = k_ref[0, pl.ds(pl.multiple_of(j * tk, tk), tk), :].astype(F32)
            kk = kk * kk
            n2 = jnp.maximum(jnp.sum(kk[:, :dh], axis=1, keepdims=True), jnp.sum(kk[:, dh:], axis=1, keepdims=True))
            return jnp.maximum(mx, jnp.max(n2, axis=0, keepdims=True))
        kn2 = lax.fori_loop(0, nk, kn_body, jnp.zeros((1, 1), F32))
        kn_scr[...] = jnp.broadcast_to(kn2, kn_scr.shape)

    qf = q.astype(F32)
    kdg = k_ref[0, pl.ds(pl.multiple_of(q0, tq), tq), :].astype(F32)
    sii = [jnp.sum(qf[:, c * dh:(c + 1) * dh] * kdg[:, c * dh:(c + 1) * dh], axis=1, keepdims=True) * c1
           for c in range(2)]
    qq = qf * qf
    qn2 = jnp.max(jnp.maximum(jnp.sum(qq[:, :dh], axis=1, keepdims=True), jnp.sum(qq[:, dh:], axis=1, keepdims=True)),
                  axis=0, keepdims=True)
    sii_min = jnp.min(jnp.minimum(sii[0], sii[1]), axis=0, keepdims=True)
    bound = jnp.sqrt(qn2 * kn_scr[0:1, 0:1]) * c1 - sii_min
    reach = jnp.minimum((bound + ATTN_SKIP_LOG2) / (slope2 * tk), float(nk))
    nd = jnp.max(jnp.floor(reach).astype(jnp.int32)) + 1
    jlo = jnp.maximum(jd - nd, 0)
    jhi = jnp.minimum(jd + nd, nk - 1)

    def load_kv(j):
        start = pl.multiple_of(j * tk, tk)
        return k_ref[0, pl.ds(start, tk), :], v_ref[0, pl.ds(start, tk), :]

    def dist_bias(j):
        return slope2 * jnp.abs((qrel + (q0 - j * tk).astype(F32)) - krel)

    l_scr[...] = jnp.zeros(l_scr.shape, F32)
    acc_scr[...] = jnp.zeros(acc_scr.shape, F32)

    def accumulate(k, v, logit_fn):
        for c in range(2):
            s = _dot_nt(q[:, c * dh:(c + 1) * dh], k[:, c * dh:(c + 1) * dh])
            p = jnp.exp2(logit_fn(c, s))
            l_scr[c] += jnp.sum(p, axis=1, keepdims=True)
            acc_scr[c] += _dot(p.astype(BF16), v)

    row_left = [-slope2 * qrel - sii[c] for c in range(2)]
    row_right = [slope2 * qrel - sii[c] for c in range(2)]

    def left_body(j, carry):
        k, v = load_kv(j)
        col = slope2 * (krel - (q0 - j * tk).astype(F32))
        accumulate(k, v, lambda c, s: s * c1 + col + row_left[c])
        return carry

    def right_body(j, carry):
        k, v = load_kv(j)
        col = -slope2 * (krel + (j * tk - q0).astype(F32))
        accumulate(k, v, lambda c, s: s * c1 + col + row_right[c])
        return carry

    lax.fori_loop(jlo, jd, left_body, 0)
    k_diag, v_diag = load_kv(jd)
    bias_diag = dist_bias(jd)
    accumulate(k_diag, v_diag, lambda c, s: s * c1 - bias_diag - sii[c])
    lax.fori_loop(jd + 1, jhi + 1, right_body, 0)

    bad = jnp.maximum(jnp.max(jnp.where(l_scr[...] <= 2.0 ** ATTN_REDO_LOG2, 0.0, 1.0)),
                      jnp.max(jnp.where(jnp.isfinite(acc_scr[...]), 0.0, 1.0)))

    @pl.when(bad > 0.0)
    def _():
        m_scr[...] = jnp.full(m_scr.shape, NEG_BIG, F32)
        l_scr[...] = jnp.zeros(l_scr.shape, F32)
        acc_scr[...] = jnp.zeros(acc_scr.shape, F32)

        def body(j, carry):
            k, v = load_kv(j)
            bias = dist_bias(j)
            for c in range(2):
                s = _dot_nt(q[:, c * dh:(c + 1) * dh], k[:, c * dh:(c + 1) * dh]) * c1 - bias
                m_old = m_scr[c]
                m_new = jnp.maximum(m_old, jnp.max(s, axis=1, keepdims=True))
                alpha = jnp.exp2(m_old - m_new)
                p = jnp.exp2(s - m_new[:, :1])
                l_scr[c] = alpha * l_scr[c] + jnp.sum(p, axis=1, keepdims=True)
                acc_scr[c] = alpha[:, :1] * acc_scr[c] + _dot(p.astype(BF16), v)
                m_scr[c] = m_new
            return carry

        lax.fori_loop(jlo, jhi + 1, body, 0)

    lq = lq_ref[...]
    lam = (jnp.exp(jnp.sum(lq[0:1] * lq[1:2], axis=1, keepdims=True))
           - jnp.exp(jnp.sum(lq[2:3] * lq[3:4], axis=1, keepdims=True)) + lam_init)
    o = acc_scr[0] / l_scr[0][:, :1] - lam * (acc_scr[1] / l_scr[1][:, :1])
    ms = jnp.mean(o * o, axis=1, keepdims=True)
    o_ref[0] = (o * lax.rsqrt(ms + RMS_EPS) * w_ref[...] * (1.0 - lam_init)).astype(o_ref.dtype)


def _diff_attention(qkv, lambda_qk, subln_w, lam_init):
    bsz, s, _ = qkv.shape
    hw = 2 * ATTN_HEAD_DIM
    tq = _tile(s, 512)
    tk = _tile(s, 1024)
    assert tk % tq == 0
    return pl.pallas_call(
        functools.partial(_attn_kernel, tq=tq, tk=tk, lam_init=lam_init),
        grid=(bsz, ATTN_HEADS, s // tq),
        in_specs=[pl.BlockSpec((4, ATTN_HEAD_DIM), lambda b, h, i: (0, 0)),
                  pl.BlockSpec((1, hw), lambda b, h, i: (0, 0)),
                  pl.BlockSpec((1, tq, hw), lambda b, h, i: (b, i, h)),
                  pl.BlockSpec((1, s, hw), lambda b, h, i: (b, 0, ATTN_HEADS + h)),
                  pl.BlockSpec((1, s, hw), lambda b, h, i: (b, 0, 2 * ATTN_HEADS + h))],
        out_specs=pl.BlockSpec((1, tq, hw), lambda b, h, i: (b, i, h)),
        out_shape=jax.ShapeDtypeStruct((bsz, s, ATTN_HEADS * hw), BF16),
        scratch_shapes=[pltpu.VMEM((2, tq, LANES), F32),
                        pltpu.VMEM((2, tq, LANES), F32),
                        pltpu.VMEM((2, tq, hw), F32),
                        pltpu.VMEM((8, LANES), F32)],
        compiler_params=_cparams("parallel", "parallel", "arbitrary"),
        name="diff_attn",
    )(lambda_qk, subln_w.reshape(1, hw), qkv, qkv, qkv)


def _conv_kernel(prev_ref, cur_ref, next_ref, w_ref, o_ref, *, ts, halo):
    i = pl.program_id(1)
    p = pl.program_id(2)
    ns = pl.num_programs(1)
    dh = DN_HEAD_DIM
    x = cur_ref[0].astype(F32)
    pv = jnp.where(i > 0, prev_ref[0].astype(F32), 0.0)
    nx = jnp.where(i < ns - 1, next_ref[0].astype(F32), 0.0)
    w = w_ref[...]
    row = lax.broadcasted_iota(jnp.int32, (ts, 1), 0)
    xm1 = jnp.where(row == 0, pv[halo - 1:halo], pltpu.roll(x, 1, 0))
    xm2 = jnp.where(row == 0, pv[halo - 2:halo - 1], jnp.where(row == 1, pv[halo - 1:halo], pltpu.roll(x, 2, 0)))
    xp1 = jnp.where(row == ts - 1, nx[0:1], pltpu.roll(x, ts - 1, 0))
    xp2 = jnp.where(row == ts - 2, nx[0:1], jnp.where(row == ts - 1, nx[1:2], pltpu.roll(x, ts - 2, 0)))
    y = w[0:1] * xm2 + w[1:2] * xm1 + w[2:3] * x + w[3:4] * xp1 + w[4:5] * xp2
    y = y * _sigmoid(y)
    qscale = jnp.where(p == 0, dh ** -0.5, 1.0)
    for hh in range(y.shape[1] // dh):
        seg = y[:, hh * dh:(hh + 1) * dh]
        nrm = lax.rsqrt(jnp.sum(seg * seg, axis=1, keepdims=True) + RMS_EPS) * qscale
        fac = jnp.where(p == 2, 1.0, nrm)
        o_ref[0, :, hh * dh:(hh + 1) * dh] = (seg * fac).astype(o_ref.dtype)


def _dn_conv(dn, conv_w):
    bsz, s, _ = dn.shape
    cw = DN_HEADS * DN_HEAD_DIM
    ts = _tile(s, 256)
    halo = 16
    hb = ts // halo
    nhb = s // halo
    return pl.pallas_call(
        functools.partial(_conv_kernel, ts=ts, halo=halo),
        grid=(bsz, s // ts, 3),
        in_specs=[pl.BlockSpec((1, halo, cw), lambda b, i, p: (b, jnp.maximum(i * hb - 1, 0), p)),
                  pl.BlockSpec((1, ts, cw), lambda b, i, p: (b, i, p)),
                  pl.BlockSpec((1, halo, cw), lambda b, i, p: (b, jnp.minimum((i + 1) * hb, nhb - 1), p)),
                  pl.BlockSpec((CONV_K, cw), lambda b, i, p: (0, p))],
        out_specs=pl.BlockSpec((1, ts, cw), lambda b, i, p: (b, i, p)),
        out_shape=jax.ShapeDtypeStruct((bsz, s, 3 * cw), BF16),
        compiler_params=_cparams("parallel", "parallel", "parallel"),
        name="dn_conv",
    )(dn, dn, dn, conv_w)


def _split3(x):
    hi = x.astype(BF16)
    r1 = x - hi.astype(F32)
    mid = r1.astype(BF16)
    lo = (r1 - mid.astype(F32)).astype(BF16)
    return hi, mid, lo


def _softplus(x):
    return jnp.maximum(x, 0.0) + jnp.log1p(jnp.exp(-jnp.abs(x)))


def _gdn_masks(rev):
    r = lax.broadcasted_iota(jnp.int32, (SUPER, SUPER), 0)
    c = lax.broadcasted_iota(jnp.int32, (SUPER, SUPER), 1)
    same = (r // CHUNK) == (c // CHUNK)
    incl = jnp.logical_and(same, (c >= r) if rev else (c <= r))
    strict = jnp.logical_and(same, (c > r) if rev else (c < r))
    return dict(
        incl=incl, strict=strict,
        incl01=jnp.where(incl, 1.0, 0.0).astype(BF16),
        eye=jnp.where(r == c, 1.0, 0.0),
        lvl=31 - lax.clz(lax.bitwise_xor(r, c)))


def _gdn_superchunks(ps):
    dh = DN_HEAD_DIM
    n = range(len(ps))
    mk = [p["mk"] for p in ps]
    kf = [p["k"].astype(F32) for p in ps]
    g_col = [jnp.broadcast_to(p["neg_a"] * _softplus(p["da_col"] + p["dtb"]), (SUPER, dh)) for p in ps]
    beta = [jnp.broadcast_to(_sigmoid(p["db_col"]), (SUPER, dh)) for p in ps]
    gc = [sum(_dot(mk[i]["incl01"], t) for t in _split3(g_col[i])) for i in n]
    tot = [jnp.concatenate([jnp.broadcast_to(gc[i][r:r + 1], (CHUNK, dh))
                            for r in range(0 if ps[i]["rev"] else CHUNK - 1, SUPER, CHUNK)], axis=0) for i in n]
    gc_row = [jnp.transpose(gc[i])[0:1] for i in n]
    eg = [jnp.exp(g) for g in gc]
    ekd = [jnp.exp(tot[i] - gc[i]) for i in n]
    decay = [jnp.exp(jnp.where(mk[i]["incl"], jnp.concatenate([gc[i], gc[i]], axis=1) - gc_row[i], NEG_BIG))
             for i in n]
    kb = [kf[i] * beta[i] for i in n]
    vb = [ps[i]["v"].astype(F32) * beta[i] for i in n]
    mm = [jnp.where(mk[i]["strict"], _dot_nt(kb[i].astype(BF16), ps[i]["k"]) * decay[i], 0.0) for i in n]
    t_inv = [mk[i]["eye"] - jnp.where(mk[i]["lvl"] == 0, mm[i], 0.0) for i in n]
    for j in range(1, 6):
        tb = [t.astype(BF16) for t in t_inv]
        wj = [_dot(tb[i], jnp.where(mk[i]["lvl"] == j, mm[i], 0.0).astype(BF16)) for i in n]
        t_inv = [t_inv[i] - _dot(wj[i].astype(BF16), tb[i]) for i in n]
    rhs = [jnp.concatenate([kb[i] * eg[i], vb[i]], axis=1).astype(BF16) for i in n]
    wub = [_dot(t_inv[i].astype(BF16), rhs[i]).astype(BF16) for i in n]
    qk = [jnp.where(mk[i]["incl"], _dot_nt(ps[i]["q"], ps[i]["k"]) * decay[i], 0.0) for i in n]
    ab = [_dot(qk[i].astype(BF16), wub[i]) for i in n]
    aq = [(ps[i]["q"].astype(F32) * eg[i] - ab[i][:, :dh]).astype(BF16) for i in n]
    kd = [(kf[i] * ekd[i]).astype(BF16) for i in n]
    nch = SUPER // CHUNK
    for step in range(nch):
        for i in n:
            p = ps[i]
            ch = nch - 1 - step if p["rev"] else step
            cr = slice(ch * CHUNK, (ch + 1) * CHUNK)
            st = p["s"][...]
            stb = st.astype(BF16)
            orow = slice(p["row0"] + ch * CHUNK, p["row0"] + (ch + 1) * CHUNK)
            p["o"][0, orow, p["col0"]:p["col0"] + dh] = _dot(aq[i][cr], stb) + ab[i][cr, dh:]
            pq = _dot_tn(kd[i][cr], wub[i][cr])
            egl = jnp.exp(tot[i][ch * CHUNK:ch * CHUNK + 1, :])
            p["s"][...] = egl * st - _dot(pq[:, :dh].astype(BF16), stb) + pq[:, dh:]


def _gdn_kernel(qf_ref, kf_ref, vf_ref, qb_ref, kb_ref, vb_ref, cf_ref, cb_ref, al_ref, dtb_ref,
                of_ref, ob_ref, s_scr, *, ts):
    dh = DN_HEAD_DIM
    hg = pl.program_id(1)
    nsc = ts // SUPER

    @pl.when(pl.program_id(2) == 0)
    def _():
        s_scr[...] = jnp.zeros(s_scr.shape, F32)

    lane = lax.broadcasted_iota(jnp.int32, (1, LANES), 1)
    chains = []
    for d, (q_ref, k_ref, v_ref, c_ref, o_ref) in enumerate(
            ((qf_ref, kf_ref, vf_ref, cf_ref, of_ref), (qb_ref, kb_ref, vb_ref, cb_ref, ob_ref))):
        mk = _gdn_masks(bool(d))
        for hh in range(GDN_HG):
            h = hg * GDN_HG + hh
            chains.append(dict(q=q_ref, k=k_ref, v=v_ref, c=c_ref, o=o_ref, mk=mk, d=d, hh=hh,
                               neg_a=-jnp.exp(al_ref[d, hh]), dtb=dtb_ref[d, hh],
                               asel=lane == d * DN_HEADS + h, bsel=lane == (2 + d) * DN_HEADS + h))

    for step in range(nsc):
        ps = []
        for cc in chains:
            d, hh = cc["d"], cc["hh"]
            sc = nsc - 1 - step if d else step
            rows = slice(sc * SUPER, (sc + 1) * SUPER)
            hc = slice(hh * dh, (hh + 1) * dh)
            cols = cc["c"][0, rows, :]
            ps.append(dict(q=cc["q"][0, rows, hc], k=cc["k"][0, rows, hc], v=cc["v"][0, rows, hc],
                           da_col=jnp.sum(jnp.where(cc["asel"], cols, 0.0), axis=1, keepdims=True),
                           db_col=jnp.sum(jnp.where(cc["bsel"], cols, 0.0), axis=1, keepdims=True),
                           neg_a=cc["neg_a"], dtb=cc["dtb"], mk=cc["mk"],
                           s=s_scr.at[d * GDN_HG + hh], o=cc["o"], row0=sc * SUPER, col0=hh * dh, rev=bool(d)))
        _gdn_superchunks(ps)


def _gdn(qkv, dab, a_log, dt_bias):
    bsz, s, _ = qkv.shape
    dh = DN_HEAD_DIM
    ts = _tile(s, 1024)
    n = s // ts

    def spec(shape, fn):
        return (pl.BlockSpec(shape, lambda b, h, i: fn(b, h, i)),
                pl.BlockSpec(shape, lambda b, h, i: fn(b, h, n - 1 - i)))

    ng = DN_HEADS // GDN_HG
    gw = GDN_HG * dh
    qs = spec((1, ts, gw), lambda b, h, t: (b, t, h))
    ks = spec((1, ts, gw), lambda b, h, t: (b, t, ng + h))
    vs = spec((1, ts, gw), lambda b, h, t: (b, t, 2 * ng + h))
    cs = spec((1, ts, LANES), lambda b, h, t: (b, t, 0))
    par = pl.BlockSpec((2, GDN_HG, 1, 1), lambda b, h, i: (0, h, 0, 0))
    out = jax.ShapeDtypeStruct((bsz, s, DN_HEADS * dh), F32)
    return pl.pallas_call(
        functools.partial(_gdn_kernel, ts=ts),
        grid=(bsz, ng, n),
        in_specs=[qs[0], ks[0], vs[0], qs[1], ks[1], vs[1], cs[0], cs[1], par, par],
        out_specs=list(qs),
        out_shape=[out, out],
        scratch_shapes=[pltpu.VMEM((2 * GDN_HG, dh, dh), F32)],
        compiler_params=_cparams("parallel", "parallel", "arbitrary"),
        name="gdn",
    )(qkv, qkv, qkv, qkv, qkv, qkv, dab, dab,
      a_log.reshape(2, DN_HEADS, 1, 1), dt_bias.reshape(2, DN_HEADS, 1, 1))


def _gdn_out_kernel(of_ref, ob_ref, z_ref, w_ref, o_ref):
    dh = DN_HEAD_DIM
    w = w_ref[...]
    for hh in range(DN_HEADS):
        cs = slice(hh * dh, (hh + 1) * dh)
        o = of_ref[0, :, cs] + ob_ref[0, :, cs]
        z = z_ref[0, :, cs].astype(F32)
        y = o * lax.rsqrt(jnp.mean(o * o, axis=1, keepdims=True) + RMS_EPS) * w
        o_ref[0, :, cs] = (y * (z * _sigmoid(z))).astype(o_ref.dtype)


def _gdn_out(o_f, o_b, dn, norm_w):
    bsz, s, cw = o_f.shape
    ts = _tile(s, 256)
    row = pl.BlockSpec((1, ts, cw), lambda b, i: (b, i, 0))
    return pl.pallas_call(
        _gdn_out_kernel,
        grid=(bsz, s // ts),
        in_specs=[row, row,
                  pl.BlockSpec((1, ts, cw), lambda b, i: (b, i, 3)),
                  pl.BlockSpec((1, DN_HEAD_DIM), lambda b, i: (0, 0))],
        out_specs=row,
        out_shape=jax.ShapeDtypeStruct((bsz, s, cw), BF16),
        compiler_params=_cparams("parallel", "parallel"),
        name="gdn_out",
    )(o_f, o_b, dn, norm_w.reshape(1, DN_HEAD_DIM))


def _merge_kernel(oa_ref, od_ref, wa_ref, wd_ref, ga_ref, gd_ref, o_ref):
    a = _dot(oa_ref[...], wa_ref[...])
    d = _dot(od_ref[...], wd_ref[...])
    o_ref[...] = (_sigmoid(ga_ref[...].astype(F32)) * a + _sigmoid(gd_ref[...].astype(F32)) * d).astype(o_ref.dtype)


def _merge(oa, od, wa, wd, gates):
    m, k = oa.shape
    n = wa.shape[1]
    tm, tn = _tile(m, 1024), _tile(n, 1024)
    nj = n // tn
    return pl.pallas_call(
        _merge_kernel,
        grid=(m // tm, nj),
        in_specs=[pl.BlockSpec((tm, k), lambda i, j: (i, 0)),
                  pl.BlockSpec((tm, k), lambda i, j: (i, 0)),
                  pl.BlockSpec((k, tn), lambda i, j: (0, j)),
                  pl.BlockSpec((k, tn), lambda i, j: (0, j)),
                  pl.BlockSpec((tm, tn), lambda i, j: (i, j)),
                  pl.BlockSpec((tm, tn), lambda i, j: (i, nj + j))],
        out_specs=pl.BlockSpec((tm, tn), lambda i, j: (i, j)),
        out_shape=jax.ShapeDtypeStruct((m, n), BF16),
        compiler_params=_cparams("parallel", "arbitrary"),
        name="branch_merge",
    )(oa, od, wa, wd, gates, gates)


def _topk_rows(x, payload=None):
    n = x.shape[0]
    rid = lax.broadcasted_iota(jnp.int32, x.shape, 0)
    vals, picks = [], []
    for _ in range(PEER_TOPK):
        mx = jnp.max(x, axis=0, keepdims=True)
        first = jnp.min(jnp.where(x == mx, rid, n), axis=0, keepdims=True)
        hit = rid == first
        vals.append(mx)
        picks.append(first if payload is None else jnp.max(jnp.where(hit, payload, -1), axis=0, keepdims=True))
        x = jnp.where(hit, -jnp.inf, x)
    return jnp.concatenate(vals, axis=0), jnp.concatenate(picks, axis=0)


def _route_kernel(q_ref, sk_ref, idx_ref, gate_ref):
    sv, si = [], []
    for p in range(2):
        sc = _dot_nt(sk_ref[0, p].astype(BF16), q_ref[:, p * N_KEYS:(p + 1) * N_KEYS].astype(BF16))
        v, r = _topk_rows(sc)
        sv.append(v)
        si.append(r)
    half = PEER_TOPK // 2

    def pairs(a, b, op):
        return jnp.concatenate([op(a[0:1], b)] + [op(a[k:k + 1], b[0:half]) for k in range(1, half)]
                               + [op(a[half:], b[0:1])], axis=0)

    cand = pairs(sv[0], sv[1], lambda x, y: x + y)
    expert = pairs(si[0], si[1], lambda x, y: x * N_KEYS + y)
    fv, fe = _topk_rows(cand, expert)
    e = jnp.exp(fv - fv[0:1])
    idx_ref[0] = fe
    gate_ref[0] = e / jnp.sum(e, axis=0, keepdims=True)


def _peer_route(q, subkeys):
    t, qw = q.shape
    tt = _tile(t, 256)
    hq = qw // PEER_HEADS
    out = jax.ShapeDtypeStruct((PEER_HEADS, PEER_TOPK, t), jnp.int32)
    idx_t, gate_t = pl.pallas_call(
        _route_kernel,
        grid=(t // tt, PEER_HEADS),
        in_specs=[pl.BlockSpec((tt, hq), lambda i, h: (i, h)),
                  pl.BlockSpec((1, 2, N_KEYS, hq // 2), lambda i, h: (h, 0, 0, 0))],
        out_specs=[pl.BlockSpec((1, PEER_TOPK, tt), lambda i, h: (h, 0, i)),
                   pl.BlockSpec((1, PEER_TOPK, tt), lambda i, h: (h, 0, i))],
        out_shape=[out, jax.ShapeDtypeStruct(out.shape, F32)],
        compiler_params=_cparams("parallel", "parallel"),
        name="peer_route",
    )(q, subkeys)
    ne = PEER_HEADS * PEER_TOPK
    return idx_t.reshape(ne, t).T, gate_t.reshape(ne, t).T


def _gate_kernel(idx_ref, g_ref, o_ref, *, tt):
    ne = idx_ref.shape[1]
    sub = lax.broadcasted_iota(jnp.int32, (N_KEYS, ne), 0)

    def body(t, carry):
        idx = idx_ref[pl.ds(t, 1), :]
        g = g_ref[pl.ds(t, 1), :]
        i1 = lax.shift_right_logical(idx, 7)
        i2 = lax.bitwise_and(idx, N_KEYS - 1)
        a = jnp.where(sub == i1, g, 0.0).astype(BF16)
        bt = jnp.where(sub == i2, 1.0, 0.0).astype(BF16)
        o_ref[t] = _dot_nt(a, bt).astype(o_ref.dtype)
        return carry

    lax.fori_loop(0, tt, body, 0, unroll=16)


def _peer_gates(idx, gate):
    t, ne = idx.shape
    tt = _tile(t, 128)
    return pl.pallas_call(
        functools.partial(_gate_kernel, tt=tt),
        grid=(t // tt,),
        in_specs=[pl.BlockSpec((tt, ne), lambda i: (i, 0)),
                  pl.BlockSpec((tt, ne), lambda i: (i, 0))],
        out_specs=pl.BlockSpec((tt, N_KEYS, N_KEYS), lambda i: (i, 0, 0)),
        out_shape=jax.ShapeDtypeStruct((t, N_KEYS, N_KEYS), BF16),
        compiler_params=_cparams("parallel"),
        name="peer_gates",
    )(idx, gate)


def _peer_kernel(x_ref, u_ref, v_ref, g_ref, o_ref):
    n = pl.program_id(1)

    @pl.when(n == 0)
    def _():
        o_ref[...] = jnp.zeros(o_ref.shape, F32)

    hid = _dot_nt(x_ref[...], u_ref[...])
    act = 0.5 * hid * (1.0 + lax.erf(hid * (2.0 ** -0.5)))
    wgt = (g_ref[...].astype(F32) * act).astype(BF16)
    o_ref[...] += _dot(wgt, v_ref[...])


def _peer_dense(x, u, v, g):
    t, d = x.shape
    ne = u.shape[0]
    tm = next(c for c in (384, 256, 128, t) if t % c == 0)
    tn = _tile(ne, 1024)
    return pl.pallas_call(
        _peer_kernel,
        grid=(t // tm, ne // tn),
        in_specs=[pl.BlockSpec((tm, d), lambda i, n: (i, 0)),
                  pl.BlockSpec((tn, d), lambda i, n: (n, 0)),
                  pl.BlockSpec((tn, d), lambda i, n: (n, 0)),
                  pl.BlockSpec((tm, tn), lambda i, n: (i, n))],
        out_specs=pl.BlockSpec((tm, d), lambda i, n: (i, 0)),
        out_shape=jax.ShapeDtypeStruct((t, d), F32),
        compiler_params=_cparams("parallel", "arbitrary"),
        name="peer_dense",
    )(x, u, v, g)


def _token_mixer(h, lam_init, wb, lambda_qk, attn_subln_w, dn_conv_w, dn_a_log, dn_dt_bias, dn_norm_w,
                 wb_br_attn, wb_br_dn, wb_out):
    bsz, s, d = h.shape
    t = bsz * s
    aw = ATTN_HEADS * 2 * ATTN_HEAD_DIM
    dw = DN_HEADS * DN_HEAD_DIM
    h2d = h.reshape(t, d)

    o0, o1, o3 = 3 * aw, 3 * aw + 4 * dw, 3 * aw + 4 * dw + 4 * DN_HEADS
    qkv = _matmul(h2d, wb, BF16, "proj_attn", n=o0).reshape(bsz, s, 3 * aw)
    dn = _matmul(h2d, wb, BF16, "proj_dn", n=o1 - o0, col0=o0).reshape(bsz, s, 4 * dw)
    dab = _matmul(h2d, wb, F32, "proj_ab", n=LANES, col0=o1).reshape(bsz, s, LANES)
    gates = _matmul(h2d, wb[:, o3:], BF16, "proj_gate")

    oa = _diff_attention(qkv, lambda_qk, attn_subln_w, lam_init)

    dqkv = _dn_conv(dn, dn_conv_w)
    o_f, o_b = _gdn(dqkv, dab, dn_a_log, dn_dt_bias)
    od = _gdn_out(o_f, o_b, dn, dn_norm_w)

    merged = _merge(oa.reshape(t, aw), od.reshape(t, dw), wb_br_attn, wb_br_dn, gates)
    return _matmul(merged, wb_out, F32, "out_proj").reshape(bsz, s, d)


def _peer_ffn(h, wb_q, peer_subkeys, ub, vb):
    bsz, s, d = h.shape
    t = bsz * s
    h2d = h.reshape(t, d)
    pq = _matmul(h2d, wb_q, F32, "peer_q")
    idx, gate = _peer_route(pq, peer_subkeys)
    gmat = _peer_gates(idx, gate).reshape(t, N_KEYS * N_KEYS)
    return _peer_dense(h2d, ub, vb, gmat).reshape(bsz, s, d)


def kernel(x_prompt, x_sample, c_prompt, c_sample, w_ada, b_ada, w_in, lambda_qk, attn_subln_w, dn_conv_w, dn_a_log, dn_dt_bias, dn_norm_w, w_br_attn, w_br_dn, w_out, ln1_g, ln1_b, peer_wq, peer_subkeys, peer_u, peer_v, ln2_g, ln2_b):
    depth = w_ada.shape[0]
    alpha = (2.0 * depth) ** 0.25
    nb_prompt = x_prompt.shape[0]
    x = jnp.concatenate([x_prompt, x_sample], axis=0)
    c = jnp.concatenate([c_prompt, c_sample], axis=0)
    bsz, s, d = x.shape
    c8 = jnp.zeros((8, d), F32).at[:bsz].set(c)

    mods = []
    for l in range(depth):
        mod = _ada(c8, w_ada, b_ada, l)[:bsz]
        mods.append([mod[:, None, j * d:(j + 1) * d] for j in range(6)])

    h = _lnmod(x, mods[0][1], mods[0][0])
    for l in range(depth):
        sh1, sc1, g1, sh2, sc2, g2 = mods[l]
        lam_init = 0.8 - 0.6 * math.exp(-0.3 * l)
        y = _token_mixer(h, lam_init, _to_bf16(w_in, l), lambda_qk[l], attn_subln_w[l], dn_conv_w[l], dn_a_log[l],
                         dn_dt_bias[l], dn_norm_w[l], _to_bf16(w_br_attn, l), _to_bf16(w_br_dn, l),
                         _to_bf16(w_out, l))
        x, h = _res_ln(x, y, g1, ln1_g[l], ln1_b[l], alpha, sc2, sh2)
        y = _peer_ffn(h, _to_bf16(peer_wq, l), peer_subkeys[l], _to_bf16(peer_u, l), _to_bf16(peer_v, l))
        if l + 1 < depth:
            x, h = _res_ln(x, y, g2, ln2_g[l], ln2_b[l], alpha, mods[l + 1][1], mods[l + 1][0])

    l = depth - 1
    g2 = mods[l][5]
    y_prompt, _ = _res_ln(x, y, g2, ln2_g[l], ln2_b[l], alpha, b0=0, nb=nb_prompt)
    y_sample, _ = _res_ln(x, y, g2, ln2_g[l], ln2_b[l], alpha, b0=nb_prompt, nb=bsz - nb_prompt)
    return y_prompt, y_sample
```

```python
import functools
import math

import jax
import jax.numpy as jnp
from jax import lax
from jax.experimental import pallas as pl
from jax.experimental.pallas import tpu as pltpu

F32 = jnp.float32
BF16 = jnp.bfloat16

V7X_VMEM_BYTES = 64 * 1024 * 1024
VMEM_LIMIT = V7X_VMEM_BYTES - 6 * 1024 * 1024
LANES = 128

ATTN_HEADS = 8
ATTN_HEAD_DIM = 128
DN_HEADS = 16
DN_HEAD_DIM = 128
CONV_K = 5
CHUNK = 64
SUPER = 256
GDN_HG = 2
PEER_HEADS = 8
N_KEYS = 128
PEER_TOPK = 16
LN_EPS = 1e-5
RMS_EPS = 1e-6
LOG2E = 1.4426950408889634
NEG_BIG = -1e30


def _cparams(*sem):
    return pltpu.CompilerParams(dimension_semantics=sem, vmem_limit_bytes=VMEM_LIMIT)


def _tile(n, pref):
    t = min(n, pref)
    assert n % t == 0, (n, pref)
    return t


def _dot(a, b):
    return jnp.dot(a, b, preferred_element_type=F32)


def _dot_nt(a, b):
    return lax.dot_general(a, b, (((1,), (1,)), ((), ())), preferred_element_type=F32)


def _dot_tn(a, b):
    return lax.dot_general(a, b, (((0,), (0,)), ((), ())), preferred_element_type=F32)


def _sigmoid(x):
    return 1.0 / (1.0 + jnp.exp(-x))


def _layer_norm(x):
    mu = jnp.mean(x, axis=-1, keepdims=True)
    xc = x - mu
    var = jnp.mean(xc * xc, axis=-1, keepdims=True)
    return xc * lax.rsqrt(var + LN_EPS)


def _mm_kernel(a_ref, b_ref, o_ref):
    o_ref[...] = _dot(a_ref[...], b_ref[...]).astype(o_ref.dtype)


def _cast_kernel(x_ref, o_ref):
    o_ref[...] = x_ref[...].astype(o_ref.dtype)


def _to_bf16(w, l):
    _, r, c = w.shape
    tr = _tile(r, max(16, (1 << 20) // c // 16 * 16))
    return pl.pallas_call(
        _cast_kernel,
        grid=(r // tr,),
        in_specs=[pl.BlockSpec((None, tr, c), lambda i: (l, i, 0))],
        out_specs=pl.BlockSpec((tr, c), lambda i: (i, 0)),
        out_shape=jax.ShapeDtypeStruct((r, c), BF16),
        compiler_params=_cparams("parallel"),
        name="to_bf16",
    )(w)


def _matmul(a, b, out_dtype, name, tm=1024, tn=1024, n=None, col0=0):
    m, k = a.shape
    n = b.shape[1] if n is None else n
    tm, tn = _tile(m, tm), _tile(n, tn)
    assert col0 % tn == 0
    jo = col0 // tn
    return pl.pallas_call(
        _mm_kernel,
        grid=(m // tm, n // tn),
        in_specs=[pl.BlockSpec((tm, k), lambda i, j: (i, 0)),
                  pl.BlockSpec((k, tn), lambda i, j: (0, jo + j))],
        out_specs=pl.BlockSpec((tm, tn), lambda i, j: (i, j)),
        out_shape=jax.ShapeDtypeStruct((m, n), out_dtype),
        compiler_params=_cparams("parallel", "arbitrary"),
        name=name,
    )(a, b)


def _ada_kernel(c_ref, w_ref, b_ref, o_ref):
    c = c_ref[...]
    s = (c * _sigmoid(c)).astype(BF16)
    o_ref[...] = _dot(s, w_ref[...].astype(BF16)) + b_ref[...]


def _ada(c8, w, b, l):
    depth, d, n = w.shape
    tn = _tile(n, 512)
    return pl.pallas_call(
        _ada_kernel,
        grid=(n // tn,),
        in_specs=[pl.BlockSpec((8, d), lambda j: (0, 0)),
                  pl.BlockSpec((None, d, tn), lambda j: (l, 0, j)),
                  pl.BlockSpec((None, 1, tn), lambda j: (l, 0, j))],
        out_specs=pl.BlockSpec((8, tn), lambda j: (0, j)),
        out_shape=jax.ShapeDtypeStruct((8, n), F32),
        compiler_params=_cparams("arbitrary"),
        name="ada_mod",
    )(c8, w, b.reshape(depth, 1, n))


def _lnmod_kernel(x_ref, sc_ref, sh_ref, h_ref):
    h_ref[0] = (_layer_norm(x_ref[0]) * (1.0 + sc_ref[0]) + sh_ref[0]).astype(h_ref.dtype)


def _lnmod(x, sc, sh):
    bsz, s, d = x.shape
    tr = _tile(s, 256)
    row = pl.BlockSpec((1, tr, d), lambda b, i: (b, i, 0))
    vec = pl.BlockSpec((1, 1, d), lambda b, i: (b, 0, 0))
    return pl.pallas_call(
        _lnmod_kernel,
        grid=(bsz, s // tr),
        in_specs=[row, vec, vec],
        out_specs=row,
        out_shape=jax.ShapeDtypeStruct((bsz, s, d), BF16),
        compiler_params=_cparams("parallel", "parallel"),
        name="ln_mod",
    )(x, sc, sh)


def _res_ln_kernel(x_ref, y_ref, g_ref, lg_ref, lb_ref, *rest, alpha, with_h):
    z = alpha * x_ref[0] + g_ref[0] * y_ref[0].astype(F32)
    x1 = _layer_norm(z) * lg_ref[...] + lb_ref[...]
    if with_h:
        sc_ref, sh_ref, xo_ref, h_ref = rest
        h_ref[0] = (_layer_norm(x1) * (1.0 + sc_ref[0]) + sh_ref[0]).astype(h_ref.dtype)
    else:
        (xo_ref,) = rest
    xo_ref[0] = x1


def _res_ln(x, y, gate, ln_g, ln_b, alpha, sc=None, sh=None, b0=0, nb=None):
    bsz, s, d = x.shape
    nb = bsz - b0 if nb is None else nb
    tr = _tile(s, 256)
    row_in = pl.BlockSpec((1, tr, d), lambda b, i: (b0 + b, i, 0))
    row_out = pl.BlockSpec((1, tr, d), lambda b, i: (b, i, 0))
    vec = pl.BlockSpec((1, 1, d), lambda b, i: (b0 + b, 0, 0))
    par = pl.BlockSpec((1, d), lambda b, i: (0, 0))
    with_h = sc is not None
    ins = [x, y, gate, ln_g.reshape(1, d), ln_b.reshape(1, d)]
    in_specs = [row_in, row_in, vec, par, par]
    out_shape = [jax.ShapeDtypeStruct((nb, s, d), F32)]
    out_specs = [row_out]
    if with_h:
        ins += [sc, sh]
        in_specs += [vec, vec]
        out_shape.append(jax.ShapeDtypeStruct((nb, s, d), BF16))
        out_specs.append(row_out)
    out = pl.pallas_call(
        functools.partial(_res_ln_kernel, alpha=alpha, with_h=with_h),
        grid=(nb, s // tr),
        in_specs=in_specs,
        out_specs=out_specs,
        out_shape=out_shape,
        compiler_params=_cparams("parallel", "parallel"),
        name="res_ln",
    )(*ins)
    return (out[0], out[1]) if with_h else (out[0], None)


ATTN_SKIP_LOG2 = 130.0
ATTN_REDO_LOG2 = 60.0


def _attn_kernel(lq_ref, w_ref, q_ref, k_ref, v_ref, o_ref, m_scr, l_scr, acc_scr, kn_scr, *, tq, tk, lam_init):
    dh = ATTN_HEAD_DIM
    h = pl.program_id(1)
    i = pl.program_id(2)
    s_len = k_ref.shape[1]
    nk = s_len // tk
    slope2 = jnp.exp2(-(jnp.full((1, 1), h + 1, jnp.int32).astype(F32))) * LOG2E
    c1 = dh ** -0.5 * LOG2E
    q0 = i * tq
    jd = lax.div(q0, tk)
    q = q_ref[0]
    qrel = lax.broadcasted_iota(jnp.int32, (tq, 1), 0).astype(F32)
    krel = lax.broadcasted_iota(jnp.int32, (1, tk), 1).astype(F32)

    @pl.when(i == 0)
    def _():
        def kn_body(j, mx):
            kk = k_ref[0, pl.ds(pl.multiple_of(j * tk, tk), tk), :].astype(F32)
            kk = kk * kk
            n2 = jnp.maximum(jnp.sum(kk[:, :dh], axis=1, keepdims=True), jnp.sum(kk[:, dh:], axis=1, keepdims=True))
            return jnp.maximum(mx, jnp.max(n2, axis=0, keepdims=True))
        kn2 = lax.fori_loop(0, nk, kn_body, jnp.zeros((1, 1), F32))
        kn_scr[...] = jnp.broadcast_to(kn2, kn_scr.shape)

    qf = q.astype(F32)
    kdg = k_ref[0, pl.ds(pl.multiple_of(q0, tq), tq), :].astype(F32)
    sii = [jnp.sum(qf[:, c * dh:(c + 1) * dh] * kdg[:, c * dh:(c + 1) * dh], axis=1, keepdims=True) * c1
           for c in range(2)]
    qq = qf * qf
    qn2 = jnp.max(jnp.maximum(jnp.sum(qq[:, :dh], axis=1, keepdims=True), jnp.sum(qq[:, dh:], axis=1, keepdims=True)),
                  axis=0, keepdims=True)
    sii_min = jnp.min(jnp.minimum(sii[0], sii[1]), axis=0, keepdims=True)
    bound = jnp.sqrt(qn2 * kn_scr[0:1, 0:1]) * c1 - sii_min
    reach = jnp.minimum((bound + ATTN_SKIP_LOG2) / (slope2 * tk), float(nk))
    nd = jnp.max(jnp.floor(reach).astype(jnp.int32)) + 1
    jlo = jnp.maximum(jd - nd, 0)
    jhi = jnp.minimum(jd + nd, nk - 1)

    def load_kv(j):
        start = pl.multiple_of(j * tk, tk)
        return k_ref[0, pl.ds(start, tk), :], v_ref[0, pl.ds(start, tk), :]

    def dist_bias(j):
        return slope2 * jnp.abs((qrel + (q0 - j * tk).astype(F32)) - krel)

    l_scr[...] = jnp.zeros(l_scr.shape, F32)
    acc_scr[...] = jnp.zeros(acc_scr.shape, F32)

    def accumulate(k, v, logit_fn):
        for c in range(2):
            s = _dot_nt(q[:, c * dh:(c + 1) * dh], k[:, c * dh:(c + 1) * dh])
            p = jnp.exp2(logit_fn(c, s))
            l_scr[c] += jnp.sum(p, axis=1, keepdims=True)
            acc_scr[c] += _dot(p.astype(BF16), v)

    row_left = [-slope2 * qrel - sii[c] for c in range(2)]
    row_right = [slope2 * qrel - sii[c] for c in range(2)]

    def left_body(j, carry):
        k, v = load_kv(j)
        col = slope2 * (krel - (q0 - j * tk).astype(F32))
        accumulate(k, v, lambda c, s: s * c1 + col + row_left[c])
        return carry

    def right_body(j, carry):
        k, v = load_kv(j)
        col = -slope2 * (krel + (j * tk - q0).astype(F32))
        accumulate(k, v, lambda c, s: s * c1 + col + row_right[c])
        return carry

    lax.fori_loop(jlo, jd, left_body, 0)
    k_diag, v_diag = load_kv(jd)
    bias_diag = dist_bias(jd)
    accumulate(k_diag, v_diag, lambda c, s: s * c1 - bias_diag - sii[c])
    lax.fori_loop(jd + 1, jhi + 1, right_body, 0)

    bad = jnp.maximum(jnp.max(jnp.where(l_scr[...] <= 2.0 ** ATTN_REDO_LOG2, 0.0, 1.0)),
                      jnp.max(jnp.where(jnp.isfinite(acc_scr[...]), 0.0, 1.0)))

    @pl.when(bad > 0.0)
    def _():
        m_scr[...] = jnp.full(m_scr.shape, NEG_BIG, F32)
        l_scr[...] = jnp.zeros(l_scr.shape, F32)
        acc_scr[...] = jnp.zeros(acc_scr.shape, F32)

        def body(j, carry):
            k, v = load_kv(j)
            bias = dist_bias(j)
            for c in range(2):
                s = _dot_nt(q[:, c * dh:(c + 1) * dh], k[:, c * dh:(c + 1) * dh]) * c1 - bias
                m_old = m_scr[c]
                m_new = jnp.maximum(m_old, jnp.max(s, axis=1, keepdims=True))
                alpha = jnp.exp2(m_old - m_new)
                p = jnp.exp2(s - m_new[:, :1])
                l_scr[c] = alpha * l_scr[c] + jnp.sum(p, axis=1, keepdims=True)
                acc_scr[c] = alpha[:, :1] * acc_scr[c] + _dot(p.astype(BF16), v)
                m_scr[c] = m_new
            return carry

        lax.fori_loop(jlo, jhi + 1, body, 0)

    lq = lq_ref[...]
    lam = (jnp.exp(jnp.sum(lq[0:1] * lq[1:2], axis=1, keepdims=True))
           - jnp.exp(jnp.sum(lq[2:3] * lq[3:4], axis=1, keepdims=True)) + lam_init)
    o = acc_scr[0] / l_scr[0][:, :1] - lam * (acc_scr[1] / l_scr[1][:, :1])
    ms = jnp.mean(o * o, axis=1, keepdims=True)
    o_ref[0] = (o * lax.rsqrt(ms + RMS_EPS) * w_ref[...] * (1.0 - lam_init)).astype(o_ref.dtype)


def _diff_attention(qkv, lambda_qk, subln_w, lam_init):
    bsz, s, _ = qkv.shape
    hw = 2 * ATTN_HEAD_DIM
    tq = _tile(s, 512)
    tk = _tile(s, 1024)
    assert tk % tq == 0
    return pl.pallas_call(
        functools.partial(_attn_kernel, tq=tq, tk=tk, lam_init=lam_init),
        grid=(bsz, ATTN_HEADS, s // tq),
        in_specs=[pl.BlockSpec((4, ATTN_HEAD_DIM), lambda b, h, i: (0, 0)),
                  pl.BlockSpec((1, hw), lambda b, h, i: (0, 0)),
                  pl.BlockSpec((1, tq, hw), lambda b, h, i: (b, i, h)),
                  pl.BlockSpec((1, s, hw), lambda b, h, i: (b, 0, ATTN_HEADS + h)),
                  pl.BlockSpec((1, s, hw), lambda b, h, i: (b, 0, 2 * ATTN_HEADS + h))],
        out_specs=pl.BlockSpec((1, tq, hw), lambda b, h, i: (b, i, h)),
        out_shape=jax.ShapeDtypeStruct((bsz, s, ATTN_HEADS * hw), BF16),
        scratch_shapes=[pltpu.VMEM((2, tq, LANES), F32),
                        pltpu.VMEM((2, tq, LANES), F32),
                        pltpu.VMEM((2, tq, hw), F32),
                        pltpu.VMEM((8, LANES), F32)],
        compiler_params=_cparams("parallel", "parallel", "arbitrary"),
        name="diff_attn",
    )(lambda_qk, subln_w.reshape(1, hw), qkv, qkv, qkv)


def _conv_kernel(prev_ref, cur_ref, next_ref, w_ref, o_ref, *, ts, halo):
    i = pl.program_id(1)
    p = pl.program_id(2)
    ns = pl.num_programs(1)
    dh = DN_HEAD_DIM
    x = cur_ref[0].astype(F32)
    pv = jnp.where(i > 0, prev_ref[0].astype(F32), 0.0)
    nx = jnp.where(i < ns - 1, next_ref[0].astype(F32), 0.0)
    w = w_ref[...]
    row = lax.broadcasted_iota(jnp.int32, (ts, 1), 0)
    xm1 = jnp.where(row == 0, pv[halo - 1:halo], pltpu.roll(x, 1, 0))
    xm2 = jnp.where(row == 0, pv[halo - 2:halo - 1], jnp.where(row == 1, pv[halo - 1:halo], pltpu.roll(x, 2, 0)))
    xp1 = jnp.where(row == ts - 1, nx[0:1], pltpu.roll(x, ts - 1, 0))
    xp2 = jnp.where(row == ts - 2, nx[0:1], jnp.where(row == ts - 1, nx[1:2], pltpu.roll(x, ts - 2, 0)))
    y = w[0:1] * xm2 + w[1:2] * xm1 + w[2:3] * x + w[3:4] * xp1 + w[4:5] * xp2
    y = y * _sigmoid(y)
    qscale = jnp.where(p == 0, dh ** -0.5, 1.0)
    for hh in range(y.shape[1] // dh):
        seg = y[:, hh * dh:(hh + 1) * dh]
        nrm = lax.rsqrt(jnp.sum(seg * seg, axis=1, keepdims=True) + RMS_EPS) * qscale
        fac = jnp.where(p == 2, 1.0, nrm)
        o_ref[0, :, hh * dh:(hh + 1) * dh] = (seg * fac).astype(o_ref.dtype)


def _dn_conv(dn, conv_w):
    bsz, s, _ = dn.shape
    cw = DN_HEADS * DN_HEAD_DIM
    ts = _tile(s, 256)
    halo = 16
    hb = ts // halo
    nhb = s // halo
    return pl.pallas_call(
        functools.partial(_conv_kernel, ts=ts, halo=halo),
        grid=(bsz, s // ts, 3),
        in_specs=[pl.BlockSpec((1, halo, cw), lambda b, i, p: (b, jnp.maximum(i * hb - 1, 0), p)),
                  pl.BlockSpec((1, ts, cw), lambda b, i, p: (b, i, p)),
                  pl.BlockSpec((1, halo, cw), lambda b, i, p: (b, jnp.minimum((i + 1) * hb, nhb - 1), p)),
                  pl.BlockSpec((CONV_K, cw), lambda b, i, p: (0, p))],
        out_specs=pl.BlockSpec((1, ts, cw), lambda b, i, p: (b, i, p)),
        out_shape=jax.ShapeDtypeStruct((bsz, s, 3 * cw), BF16),
        compiler_params=_cparams("parallel", "parallel", "parallel"),
        name="dn_conv",
    )(dn, dn, dn, conv_w)


def _split3(x):
    hi = x.astype(BF16)
    r1 = x - hi.astype(F32)
    mid = r1.astype(BF16)
    lo = (r1 - mid.astype(F32)).astype(BF16)
    return hi, mid, lo


def _softplus(x):
    return jnp.maximum(x, 0.0) + jnp.log1p(jnp.exp(-jnp.abs(x)))


def _gdn_masks(rev):
    r = lax.broadcasted_iota(jnp.int32, (SUPER, SUPER), 0)
    c = lax.broadcasted_iota(jnp.int32, (SUPER, SUPER), 1)
    same = (r // CHUNK) == (c // CHUNK)
    incl = jnp.logical_and(same, (c >= r) if rev else (c <= r))
    strict = jnp.logical_and(same, (c > r) if rev else (c < r))
    return dict(
        incl=incl, strict=strict,
        incl01=jnp.where(incl, 1.0, 0.0).astype(BF16),
        eye=jnp.where(r == c, 1.0, 0.0),
        lvl=31 - lax.clz(lax.bitwise_xor(r, c)))


def _gdn_superchunks(ps):
    dh = DN_HEAD_DIM
    n = range(len(ps))
    mk = [p["mk"] for p in ps]
    kf = [p["k"].astype(F32) for p in ps]
    g_col = [jnp.broadcast_to(p["neg_a"] * _softplus(p["da_col"] + p["dtb"]), (SUPER, dh)) for p in ps]
    beta = [jnp.broadcast_to(_sigmoid(p["db_col"]), (SUPER, dh)) for p in ps]
    gc = [sum(_dot(mk[i]["incl01"], t) for t in _split3(g_col[i])) for i in n]
    tot = [jnp.concatenate([jnp.broadcast_to(gc[i][r:r + 1], (CHUNK, dh))
                            for r in range(0 if ps[i]["rev"] else CHUNK - 1, SUPER, CHUNK)], axis=0) for i in n]
    gc_row = [jnp.transpose(gc[i])[0:1] for i in n]
    eg = [jnp.exp(g) for g in gc]
    ekd = [jnp.exp(tot[i] - gc[i]) for i in n]
    decay = [jnp.exp(jnp.where(mk[i]["incl"], jnp.concatenate([gc[i], gc[i]], axis=1) - gc_row[i], NEG_BIG))
             for i in n]
    kb = [kf[i] * beta[i] for i in n]
    vb = [ps[i]["v"].astype(F32) * beta[i] for i in n]
    mm = [jnp.where(mk[i]["strict"], _dot_nt(kb[i].astype(BF16), ps[i]["k"]) * decay[i], 0.0) for i in n]
    t_inv = [mk[i]["eye"] - jnp.where(mk[i]["lvl"] == 0, mm[i], 0.0) for i in n]
    for j in range(1, 6):
        tb = [t.astype(BF16) for t in t_inv]
        wj = [_dot(tb[i], jnp.where(mk[i]["lvl"] == j, mm[i], 0.0).astype(BF16)) for i in n]
        t_inv = [t_inv[i] - _dot(wj[i].astype(BF16), tb[i]) for i in n]
    rhs = [jnp.concatenate([kb[i] * eg[i], vb[i]], axis=1).astype(BF16) for i in n]
    wub = [_dot(t_inv[i].astype(BF16), rhs[i]).astype(BF16) for i in n]
    qk = [jnp.where(mk[i]["incl"], _dot_nt(ps[i]["q"], ps[i]["k"]) * decay[i], 0.0) for i in n]
    ab = [_dot(qk[i].astype(BF16), wub[i]) for i in n]
    aq = [(ps[i]["q"].astype(F32) * eg[i] - ab[i][:, :dh]).astype(BF16) for i in n]
    kd = [(kf[i] * ekd[i]).astype(BF16) for i in n]
    nch = SUPER // CHUNK
    for step in range(nch):
        for i in n:
            p = ps[i]
            ch = nch - 1 - step if p["rev"] else step
            cr = slice(ch * CHUNK, (ch + 1) * CHUNK)
            st = p["s"][...]
            stb = st.astype(BF16)
            orow = slice(p["row0"] + ch * CHUNK, p["row0"] + (ch + 1) * CHUNK)
            p["o"][0, orow, p["col0"]:p["col0"] + dh] = _dot(aq[i][cr], stb) + ab[i][cr, dh:]
            pq = _dot_tn(kd[i][cr], wub[i][cr])
            egl = jnp.exp(tot[i][ch * CHUNK:ch * CHUNK + 1, :])
            p["s"][...] = egl * st - _dot(pq[:, :dh].astype(BF16), stb) + pq[:, dh:]


def _gdn_kernel(qf_ref, kf_ref, vf_ref, qb_ref, kb_ref, vb_ref, cf_ref, cb_ref, al_ref, dtb_ref,
                of_ref, ob_ref, s_scr, *, ts):
    dh = DN_HEAD_DIM
    hg = pl.program_id(1)
    nsc = ts // SUPER

    @pl.when(pl.program_id(2) == 0)
    def _():
        s_scr[...] = jnp.zeros(s_scr.shape, F32)

    lane = lax.broadcasted_iota(jnp.int32, (1, LANES), 1)
    chains = []
    for d, (q_ref, k_ref, v_ref, c_ref, o_ref) in enumerate(
            ((qf_ref, kf_ref, vf_ref, cf_ref, of_ref), (qb_ref, kb_ref, vb_ref, cb_ref, ob_ref))):
        mk = _gdn_masks(bool(d))
        for hh in range(GDN_HG):
            h = hg * GDN_HG + hh
            chains.append(dict(q=q_ref, k=k_ref, v=v_ref, c=c_ref, o=o_ref, mk=mk, d=d, hh=hh,
                               neg_a=-jnp.exp(al_ref[d, hh]), dtb=dtb_ref[d, hh],
                               asel=lane == d * DN_HEADS + h, bsel=lane == (2 + d) * DN_HEADS + h))

    for step in range(nsc):
        ps = []
        for cc in chains:
            d, hh = cc["d"], cc["hh"]
            sc = nsc - 1 - step if d else step
            rows = slice(sc * SUPER, (sc + 1) * SUPER)
            hc = slice(hh * dh, (hh + 1) * dh)
            cols = cc["c"][0, rows, :]
            ps.append(dict(q=cc["q"][0, rows, hc], k=cc["k"][0, rows, hc], v=cc["v"][0, rows, hc],
                           da_col=jnp.sum(jnp.where(cc["asel"], cols, 0.0), axis=1, keepdims=True),
                           db_col=jnp.sum(jnp.where(cc["bsel"], cols, 0.0), axis=1, keepdims=True),
                           neg_a=cc["neg_a"], dtb=cc["dtb"], mk=cc["mk"],
                           s=s_scr.at[d * GDN_HG + hh], o=cc["o"], row0=sc * SUPER, col0=hh * dh, rev=bool(d)))
        _gdn_superchunks(ps)


def _gdn(qkv, dab, a_log, dt_bias):
    bsz, s, _ = qkv.shape
    dh = DN_HEAD_DIM
    ts = _tile(s, 1024)
    n = s // ts

    def spec(shape, fn):
        return (pl.BlockSpec(shape, lambda b, h, i: fn(b, h, i)),
                pl.BlockSpec(shape, lambda b, h, i: fn(b, h, n - 1 - i)))

    ng = DN_HEADS // GDN_HG
    gw = GDN_HG * dh
    qs = spec((1, ts, gw), lambda b, h, t: (b, t, h))
    ks = spec((1, ts, gw), lambda b, h, t: (b, t, ng + h))
    vs = spec((1, ts, gw), lambda b, h, t: (b, t, 2 * ng + h))
    cs = spec((1, ts, LANES), lambda b, h, t: (b, t, 0))
    par = pl.BlockSpec((2, GDN_HG, 1, 1), lambda b, h, i: (0, h, 0, 0))
    out = jax.ShapeDtypeStruct((bsz, s, DN_HEADS * dh), F32)
    return pl.pallas_call(
        functools.partial(_gdn_kernel, ts=ts),
        grid=(bsz, ng, n),
        in_specs=[qs[0], ks[0], vs[0], qs[1], ks[1], vs[1], cs[0], cs[1], par, par],
        out_specs=list(qs),
        out_shape=[out, out],
        scratch_shapes=[pltpu.VMEM((2 * GDN_HG, dh, dh), F32)],
        compiler_params=_cparams("parallel", "parallel", "arbitrary"),
        name="gdn",
    )(qkv, qkv, qkv, qkv, qkv, qkv, dab, dab,
      a_log.reshape(2, DN_HEADS, 1, 1), dt_bias.reshape(2, DN_HEADS, 1, 1))


def _gdn_out_kernel(of_ref, ob_ref, z_ref, w_ref, o_ref):
    dh = DN_HEAD_DIM
    w = w_ref[...]
    for hh in range(DN_HEADS):
        cs = slice(hh * dh, (hh + 1) * dh)
        o = of_ref[0, :, cs] + ob_ref[0, :, cs]
        z = z_ref[0, :, cs].astype(F32)
        y = o * lax.rsqrt(jnp.mean(o * o, axis=1, keepdims=True) + RMS_EPS) * w
        o_ref[0, :, cs] = (y * (z * _sigmoid(z))).astype(o_ref.dtype)


def _gdn_out(o_f, o_b, dn, norm_w):
    bsz, s, cw = o_f.shape
    ts = _tile(s, 256)
    row = pl.BlockSpec((1, ts, cw), lambda b, i: (b, i, 0))
    return pl.pallas_call(
        _gdn_out_kernel,
        grid=(bsz, s // ts),
        in_specs=[row, row,
                  pl.BlockSpec((1, ts, cw), lambda b, i: (b, i, 3)),
                  pl.BlockSpec((1, DN_HEAD_DIM), lambda b, i: (0, 0))],
        out_specs=row,
        out_shape=jax.ShapeDtypeStruct((bsz, s, cw), BF16),
        compiler_params=_cparams("parallel", "parallel"),
        name="gdn_out",
    )(o_f, o_b, dn, norm_w.reshape(1, DN_HEAD_DIM))


def _merge_kernel(oa_ref, od_ref, wa_ref, wd_ref, ga_ref, gd_ref, o_ref):
    a = _dot(oa_ref[...], wa_ref[...])
    d = _dot(od_ref[...], wd_ref[...])
    o_ref[...] = (_sigmoid(ga_ref[...].astype(F32)) * a + _sigmoid(gd_ref[...].astype(F32)) * d).astype(o_ref.dtype)


def _merge(oa, od, wa, wd, gates):
    m, k = oa.shape
    n = wa.shape[1]
    tm, tn = _tile(m, 1024), _tile(n, 1024)
    nj = n // tn
    return pl.pallas_call(
        _merge_kernel,
        grid=(m // tm, nj),
        in_specs=[pl.BlockSpec((tm, k), lambda i, j: (i, 0)),
                  pl.BlockSpec((tm, k), lambda i, j: (i, 0)),
                  pl.BlockSpec((k, tn), lambda i, j: (0, j)),
                  pl.BlockSpec((k, tn), lambda i, j: (0, j)),
                  pl.BlockSpec((tm, tn), lambda i, j: (i, j)),
                  pl.BlockSpec((tm, tn), lambda i, j: (i, nj + j))],
        out_specs=pl.BlockSpec((tm, tn), lambda i, j: (i, j)),
        out_shape=jax.ShapeDtypeStruct((m, n), BF16),
        compiler_params=_cparams("parallel", "arbitrary"),
        name="branch_merge",
    )(oa, od, wa, wd, gates, gates)


def _topk_rows(x, payload=None):
    n = x.shape[0]
    rid = lax.broadcasted_iota(jnp.int32, x.shape, 0)
    vals, picks = [], []
    for _ in range(PEER_TOPK):
        mx = jnp.max(x, axis=0, keepdims=True)
        first = jnp.min(jnp.where(x == mx, rid, n), axis=0, keepdims=True)
        hit = rid == first
        vals.append(mx)
        picks.append(first if payload is None else jnp.max(jnp.where(hit, payload, -1), axis=0, keepdims=True))
        x = jnp.where(hit, -jnp.inf, x)
    return jnp.concatenate(vals, axis=0), jnp.concatenate(picks, axis=0)


def _route_kernel(q_ref, sk_ref, idx_ref, gate_ref):
    sv, si = [], []
    for p in range(2):
        sc = _dot_nt(sk_ref[0, p].astype(BF16), q_ref[:, p * N_KEYS:(p + 1) * N_KEYS].astype(BF16))
        v, r = _topk_rows(sc)
        sv.append(v)
        si.append(r)
    half = PEER_TOPK // 2

    def pairs(a, b, op):
        return jnp.concatenate([op(a[0:1], b)] + [op(a[k:k + 1], b[0:half]) for k in range(1, half)]
                               + [op(a[half:], b[0:1])], axis=0)

    cand = pairs(sv[0], sv[1], lambda x, y: x + y)
    expert = pairs(si[0], si[1], lambda x, y: x * N_KEYS + y)
    fv, fe = _topk_rows(cand, expert)
    e = jnp.exp(fv - fv[0:1])
    idx_ref[0] = fe
    gate_ref[0] = e / jnp.sum(e, axis=0, keepdims=True)


def _peer_route(q, subkeys):
    t, qw = q.shape
    tt = _tile(t, 256)
    hq = qw // PEER_HEADS
    out = jax.ShapeDtypeStruct((PEER_HEADS, PEER_TOPK, t), jnp.int32)
    idx_t, gate_t = pl.pallas_call(
        _route_kernel,
        grid=(t // tt, PEER_HEADS),
        in_specs=[pl.BlockSpec((tt, hq), lambda i, h: (i, h)),
                  pl.BlockSpec((1, 2, N_KEYS, hq // 2), lambda i, h: (h, 0, 0, 0))],
        out_specs=[pl.BlockSpec((1, PEER_TOPK, tt), lambda i, h: (h, 0, i)),
                   pl.BlockSpec((1, PEER_TOPK, tt), lambda i, h: (h, 0, i))],
        out_shape=[out, jax.ShapeDtypeStruct(out.shape, F32)],
        compiler_params=_cparams("parallel", "parallel"),
        name="peer_route",
    )(q, subkeys)
    ne = PEER_HEADS * PEER_TOPK
    return idx_t.reshape(ne, t).T, gate_t.reshape(ne, t).T


def _gate_kernel(idx_ref, g_ref, o_ref, *, tt):
    ne = idx_ref.shape[1]
    sub = lax.broadcasted_iota(jnp.int32, (N_KEYS, ne), 0)

    def group(gi, carry):
        t0 = pl.multiple_of(gi * GATE_GROUP, GATE_GROUP)
        mats = []
        for u in range(GATE_GROUP):
            idx = idx_ref[pl.ds(t0 + u, 1), :]
            g = g_ref[pl.ds(t0 + u, 1), :]
            i1 = lax.shift_right_logical(idx, 7)
            i2 = lax.bitwise_and(idx, N_KEYS - 1)
            a = jnp.where(sub == i1, g, 0.0).astype(BF16)
            bt = jnp.where(sub == i2, 1.0, 0.0).astype(BF16)
            mats.append(_dot_nt(a, bt))
        planes = pltpu.einshape("tab->atb", jnp.stack(mats, axis=0))
        for a in range(N_KEYS):
            o_ref[pl.ds(t0, GATE_GROUP), a * N_KEYS:(a + 1) * N_KEYS] = planes[a].astype(o_ref.dtype)
        return carry

    lax.fori_loop(0, tt // GATE_GROUP, group, 0)


GATE_GROUP = 16


def _peer_gates(idx, gate):
    t, ne = idx.shape
    tt = _tile(t, 128)
    return pl.pallas_call(
        functools.partial(_gate_kernel, tt=tt),
        grid=(t // tt,),
        in_specs=[pl.BlockSpec((tt, ne), lambda i: (i, 0)),
                  pl.BlockSpec((tt, ne), lambda i: (i, 0))],
        out_specs=pl.BlockSpec((tt, N_KEYS * N_KEYS), lambda i: (i, 0)),
        out_shape=jax.ShapeDtypeStruct((t, N_KEYS * N_KEYS), BF16),
        compiler_params=_cparams("parallel"),
        name="peer_gates",
    )(idx, gate)


def _peer_kernel(x_ref, u_ref, v_ref, g_ref, o_ref):
    n = pl.program_id(1)

    @pl.when(n == 0)
    def _():
        o_ref[...] = jnp.zeros(o_ref.shape, F32)

    hid = _dot_nt(x_ref[...], u_ref[...])
    act = 0.5 * hid * (1.0 + lax.erf(hid * (2.0 ** -0.5)))
    wgt = (g_ref[...].astype(F32) * act).astype(BF16)
    o_ref[...] += _dot(wgt, v_ref[...])


def _peer_dense(x, u, v, g):
    t, d = x.shape
    ne = u.shape[0]
    tm = next(c for c in (384, 256, 128, t) if t % c == 0)
    tn = _tile(ne, 1024)
    return pl.pallas_call(
        _peer_kernel,
        grid=(t // tm, ne // tn),
        in_specs=[pl.BlockSpec((tm, d), lambda i, n: (i, 0)),
                  pl.BlockSpec((tn, d), lambda i, n: (n, 0)),
                  pl.BlockSpec((tn, d), lambda i, n: (n, 0)),
                  pl.BlockSpec((tm, tn), lambda i, n: (i, n))],
        out_specs=pl.BlockSpec((tm, d), lambda i, n: (i, 0)),
        out_shape=jax.ShapeDtypeStruct((t, d), F32),
        compiler_params=_cparams("parallel", "arbitrary"),
        name="peer_dense",
    )(x, u, v, g)


def _token_mixer(h, lam_init, wb, lambda_qk, attn_subln_w, dn_conv_w, dn_a_log, dn_dt_bias, dn_norm_w,
                 wb_br_attn, wb_br_dn, wb_out):
    bsz, s, d = h.shape
    t = bsz * s
    aw = ATTN_HEADS * 2 * ATTN_HEAD_DIM
    dw = DN_HEADS * DN_HEAD_DIM
    h2d = h.reshape(t, d)

    o0, o1, o3 = 3 * aw, 3 * aw + 4 * dw, 3 * aw + 4 * dw + 4 * DN_HEADS
    qkv = _matmul(h2d, wb, BF16, "proj_attn", n=o0).reshape(bsz, s, 3 * aw)
    dn = _matmul(h2d, wb, BF16, "proj_dn", n=o1 - o0, col0=o0).reshape(bsz, s, 4 * dw)
    dab = _matmul(h2d, wb, F32, "proj_ab", n=LANES, col0=o1).reshape(bsz, s, LANES)
    gates = _matmul(h2d, wb[:, o3:], BF16, "proj_gate")

    oa = _diff_attention(qkv, lambda_qk, attn_subln_w, lam_init)

    dqkv = _dn_conv(dn, dn_conv_w)
    o_f, o_b = _gdn(dqkv, dab, dn_a_log, dn_dt_bias)
    od = _gdn_out(o_f, o_b, dn, dn_norm_w)

    merged = _merge(oa.reshape(t, aw), od.reshape(t, dw), wb_br_attn, wb_br_dn, gates)
    return _matmul(merged, wb_out, F32, "out_proj").reshape(bsz, s, d)


def _peer_ffn(h, wb_q, peer_subkeys, ub, vb):
    bsz, s, d = h.shape
    t = bsz * s
    h2d = h.reshape(t, d)
    pq = _matmul(h2d, wb_q, F32, "peer_q")
    idx, gate = _peer_route(pq, peer_subkeys)
    gmat = _peer_gates(idx, gate)
    return _peer_dense(h2d, ub, vb, gmat).reshape(bsz, s, d)


def kernel(x_prompt, x_sample, c_prompt, c_sample, w_ada, b_ada, w_in, lambda_qk, attn_subln_w, dn_conv_w, dn_a_log, dn_dt_bias, dn_norm_w, w_br_attn, w_br_dn, w_out, ln1_g, ln1_b, peer_wq, peer_subkeys, peer_u, peer_v, ln2_g, ln2_b):
    depth = w_ada.shape[0]
    alpha = (2.0 * depth) ** 0.25
    nb_prompt = x_prompt.shape[0]
    x = jnp.concatenate([x_prompt, x_sample], axis=0)
    c = jnp.concatenate([c_prompt, c_sample], axis=0)
    bsz, s, d = x.shape
    c8 = jnp.zeros((8, d), F32).at[:bsz].set(c)

    mods = []
    for l in range(depth):
        mod = _ada(c8, w_ada, b_ada, l)[:bsz]
        mods.append([mod[:, None, j * d:(j + 1) * d] for j in range(6)])

    h = _lnmod(x, mods[0][1], mods[0][0])
    for l in range(depth):
        sh1, sc1, g1, sh2, sc2, g2 = mods[l]
        lam_init = 0.8 - 0.6 * math.exp(-0.3 * l)
        y = _token_mixer(h, lam_init, _to_bf16(w_in, l), lambda_qk[l], attn_subln_w[l], dn_conv_w[l], dn_a_log[l],
                         dn_dt_bias[l], dn_norm_w[l], _to_bf16(w_br_attn, l), _to_bf16(w_br_dn, l),
                         _to_bf16(w_out, l))
        x, h = _res_ln(x, y, g1, ln1_g[l], ln1_b[l], alpha, sc2, sh2)
        y = _peer_ffn(h, _to_bf16(peer_wq, l), peer_subkeys[l], _to_bf16(peer_u, l), _to_bf16(peer_v, l))
        if l + 1 < depth:
            x, h = _res_ln(x, y, g2, ln2_g[l], ln2_b[l], alpha, mods[l + 1][1], mods[l + 1][0])

    l = depth - 1
    g2 = mods[l][5]
    y_prompt, _ = _res_ln(x, y, g2, ln2_g[l], ln2_b[l], alpha, b0=0, nb=nb_prompt)
    y_sample, _ = _res_ln(x, y, g2, ln2_g[l], ln2_b[l], alpha, b0=nb_prompt, nb=bsz - nb_prompt)
    return y_prompt, y_sample
```

```python
import functools
import math

import jax
import jax.numpy as jnp
from jax import lax
from jax.experimental import pallas as pl
from jax.experimental.pallas import tpu as pltpu

F32 = jnp.float32
BF16 = jnp.bfloat16

V7X_VMEM_BYTES = 64 * 1024 * 1024
VMEM_LIMIT = V7X_VMEM_BYTES - 6 * 1024 * 1024
LANES = 128

ATTN_HEADS = 8
ATTN_HEAD_DIM = 128
DN_HEADS = 16
DN_HEAD_DIM = 128
CONV_K = 5
CHUNK = 64
SUPER = 256
GDN_HG = 2
PEER_HEADS = 8
N_KEYS = 128
PEER_TOPK = 16
LN_EPS = 1e-5
RMS_EPS = 1e-6
LOG2E = 1.4426950408889634
NEG_BIG = -1e30


def _cparams(*sem):
    return pltpu.CompilerParams(dimension_semantics=sem, vmem_limit_bytes=VMEM_LIMIT)


def _tile(n, pref):
    t = min(n, pref)
    assert n % t == 0, (n, pref)
    return t


def _dot(a, b):
    return jnp.dot(a, b, preferred_element_type=F32)


def _dot_nt(a, b):
    return lax.dot_general(a, b, (((1,), (1,)), ((), ())), preferred_element_type=F32)


def _dot_tn(a, b):
    return lax.dot_general(a, b, (((0,), (0,)), ((), ())), preferred_element_type=F32)


def _sigmoid(x):
    return 1.0 / (1.0 + jnp.exp(-x))


def _layer_norm(x):
    mu = jnp.mean(x, axis=-1, keepdims=True)
    xc = x - mu
    var = jnp.mean(xc * xc, axis=-1, keepdims=True)
    return xc * lax.rsqrt(var + LN_EPS)


def _mm_kernel(a_ref, b_ref, o_ref):
    o_ref[...] = _dot(a_ref[...], b_ref[...]).astype(o_ref.dtype)


def _cast_kernel(x_ref, o_ref):
    o_ref[...] = x_ref[...].astype(o_ref.dtype)


def _to_bf16(w, l):
    _, r, c = w.shape
    tr = _tile(r, max(16, (1 << 20) // c // 16 * 16))
    return pl.pallas_call(
        _cast_kernel,
        grid=(r // tr,),
        in_specs=[pl.BlockSpec((None, tr, c), lambda i: (l, i, 0))],
        out_specs=pl.BlockSpec((tr, c), lambda i: (i, 0)),
        out_shape=jax.ShapeDtypeStruct((r, c), BF16),
        compiler_params=_cparams("parallel"),
        name="to_bf16",
    )(w)


def _matmul(a, b, out_dtype, name, tm=1024, tn=1024, n=None, col0=0):
    m, k = a.shape
    n = b.shape[1] if n is None else n
    tm, tn = _tile(m, tm), _tile(n, tn)
    assert col0 % tn == 0
    jo = col0 // tn
    return pl.pallas_call(
        _mm_kernel,
        grid=(m // tm, n // tn),
        in_specs=[pl.BlockSpec((tm, k), lambda i, j: (i, 0)),
                  pl.BlockSpec((k, tn), lambda i, j: (0, jo + j))],
        out_specs=pl.BlockSpec((tm, tn), lambda i, j: (i, j)),
        out_shape=jax.ShapeDtypeStruct((m, n), out_dtype),
        compiler_params=_cparams("parallel", "arbitrary"),
        name=name,
    )(a, b)


def _ada_kernel(c_ref, w_ref, b_ref, o_ref):
    c = c_ref[...]
    s = (c * _sigmoid(c)).astype(BF16)
    o_ref[...] = _dot(s, w_ref[...].astype(BF16)) + b_ref[...]


def _ada(c8, w, b, l):
    depth, d, n = w.shape
    tn = _tile(n, 512)
    return pl.pallas_call(
        _ada_kernel,
        grid=(n // tn,),
        in_specs=[pl.BlockSpec((8, d), lambda j: (0, 0)),
                  pl.BlockSpec((None, d, tn), lambda j: (l, 0, j)),
                  pl.BlockSpec((None, 1, tn), lambda j: (l, 0, j))],
        out_specs=pl.BlockSpec((8, tn), lambda j: (0, j)),
        out_shape=jax.ShapeDtypeStruct((8, n), F32),
        compiler_params=_cparams("arbitrary"),
        name="ada_mod",
    )(c8, w, b.reshape(depth, 1, n))


def _lnmod_kernel(x_ref, sc_ref, sh_ref, h_ref):
    h_ref[0] = (_layer_norm(x_ref[0]) * (1.0 + sc_ref[0]) + sh_ref[0]).astype(h_ref.dtype)


def _lnmod(x, sc, sh):
    bsz, s, d = x.shape
    tr = _tile(s, 256)
    row = pl.BlockSpec((1, tr, d), lambda b, i: (b, i, 0))
    vec = pl.BlockSpec((1, 1, d), lambda b, i: (b, 0, 0))
    return pl.pallas_call(
        _lnmod_kernel,
        grid=(bsz, s // tr),
        in_specs=[row, vec, vec],
        out_specs=row,
        out_shape=jax.ShapeDtypeStruct((bsz, s, d), BF16),
        compiler_params=_cparams("parallel", "parallel"),
        name="ln_mod",
    )(x, sc, sh)


def _res_ln_kernel(x_ref, y_ref, g_ref, lg_ref, lb_ref, *rest, alpha, with_h):
    z = alpha * x_ref[0] + g_ref[0] * y_ref[0].astype(F32)
    x1 = _layer_norm(z) * lg_ref[...] + lb_ref[...]
    if with_h:
        sc_ref, sh_ref, xo_ref, h_ref = rest
        h_ref[0] = (_layer_norm(x1) * (1.0 + sc_ref[0]) + sh_ref[0]).astype(h_ref.dtype)
    else:
        (xo_ref,) = rest
    xo_ref[0] = x1


def _res_ln(x, y, gate, ln_g, ln_b, alpha, sc=None, sh=None, b0=0, nb=None):
    bsz, s, d = x.shape
    nb = bsz - b0 if nb is None else nb
    tr = _tile(s, 256)
    row_in = pl.BlockSpec((1, tr, d), lambda b, i: (b0 + b, i, 0))
    row_out = pl.BlockSpec((1, tr, d), lambda b, i: (b, i, 0))
    vec = pl.BlockSpec((1, 1, d), lambda b, i: (b0 + b, 0, 0))
    par = pl.BlockSpec((1, d), lambda b, i: (0, 0))
    with_h = sc is not None
    ins = [x, y, gate, ln_g.reshape(1, d), ln_b.reshape(1, d)]
    in_specs = [row_in, row_in, vec, par, par]
    out_shape = [jax.ShapeDtypeStruct((nb, s, d), F32)]
    out_specs = [row_out]
    if with_h:
        ins += [sc, sh]
        in_specs += [vec, vec]
        out_shape.append(jax.ShapeDtypeStruct((nb, s, d), BF16))
        out_specs.append(row_out)
    out = pl.pallas_call(
        functools.partial(_res_ln_kernel, alpha=alpha, with_h=with_h),
        grid=(nb, s // tr),
        in_specs=in_specs,
        out_specs=out_specs,
        out_shape=out_shape,
        compiler_params=_cparams("parallel", "parallel"),
        name="res_ln",
    )(*ins)
    return (out[0], out[1]) if with_h else (out[0], None)


ATTN_SKIP_LOG2 = 130.0
ATTN_REDO_LOG2 = 60.0


def _attn_kernel(lq_ref, w_ref, q_ref, k_ref, v_ref, o_ref, m_scr, l_scr, acc_scr, kn_scr, *, tq, tk, lam_init):
    dh = ATTN_HEAD_DIM
    h = pl.program_id(1)
    i = pl.program_id(2)
    s_len = k_ref.shape[1]
    nk = s_len // tk
    slope2 = jnp.exp2(-(jnp.full((1, 1), h + 1, jnp.int32).astype(F32))) * LOG2E
    c1 = dh ** -0.5 * LOG2E
    q0 = i * tq
    jd = lax.div(q0, tk)
    q = q_ref[0]
    qrel = lax.broadcasted_iota(jnp.int32, (tq, 1), 0).astype(F32)
    krel = lax.broadcasted_iota(jnp.int32, (1, tk), 1).astype(F32)

    @pl.when(i == 0)
    def _():
        def kn_body(j, mx):
            kk = k_ref[0, pl.ds(pl.multiple_of(j * tk, tk), tk), :].astype(F32)
            kk = kk * kk
            n2 = jnp.maximum(jnp.sum(kk[:, :dh], axis=1, keepdims=True), jnp.sum(kk[:, dh:], axis=1, keepdims=True))
            return jnp.maximum(mx, jnp.max(n2, axis=0, keepdims=True))
        kn2 = lax.fori_loop(0, nk, kn_body, jnp.zeros((1, 1), F32))
        kn_scr[...] = jnp.broadcast_to(kn2, kn_scr.shape)

    qf = q.astype(F32)
    kdg = k_ref[0, pl.ds(pl.multiple_of(q0, tq), tq), :].astype(F32)
    sii = [jnp.sum(qf[:, c * dh:(c + 1) * dh] * kdg[:, c * dh:(c + 1) * dh], axis=1, keepdims=True) * c1
           for c in range(2)]
    qq = qf * qf
    qn2 = jnp.max(jnp.maximum(jnp.sum(qq[:, :dh], axis=1, keepdims=True), jnp.sum(qq[:, dh:], axis=1, keepdims=True)),
                  axis=0, keepdims=True)
    sii_min = jnp.min(jnp.minimum(sii[0], sii[1]), axis=0, keepdims=True)
    bound = jnp.sqrt(qn2 * kn_scr[0:1, 0:1]) * c1 - sii_min
    reach = jnp.minimum((bound + ATTN_SKIP_LOG2) / (slope2 * tk), float(nk))
    nd = jnp.max(jnp.floor(reach).astype(jnp.int32)) + 1
    jlo = jnp.maximum(jd - nd, 0)
    jhi = jnp.minimum(jd + nd, nk - 1)

    def load_kv(j):
        start = pl.multiple_of(j * tk, tk)
        return k_ref[0, pl.ds(start, tk), :], v_ref[0, pl.ds(start, tk), :]

    def dist_bias(j):
        return slope2 * jnp.abs((qrel + (q0 - j * tk).astype(F32)) - krel)

    l_scr[...] = jnp.zeros(l_scr.shape, F32)
    acc_scr[...] = jnp.zeros(acc_scr.shape, F32)

    def accumulate(k, v, logit_fn):
        for c in range(2):
            s = _dot_nt(q[:, c * dh:(c + 1) * dh], k[:, c * dh:(c + 1) * dh])
            p = jnp.exp2(logit_fn(c, s))
            l_scr[c] += jnp.sum(p, axis=1, keepdims=True)
            acc_scr[c] += _dot(p.astype(BF16), v)

    row_left = [-slope2 * qrel - sii[c] for c in range(2)]
    row_right = [slope2 * qrel - sii[c] for c in range(2)]

    def left_body(j, carry):
        k, v = load_kv(j)
        col = slope2 * (krel - (q0 - j * tk).astype(F32))
        accumulate(k, v, lambda c, s: s * c1 + col + row_left[c])
        return carry

    def right_body(j, carry):
        k, v = load_kv(j)
        col = -slope2 * (krel + (j * tk - q0).astype(F32))
        accumulate(k, v, lambda c, s: s * c1 + col + row_right[c])
        return carry

    lax.fori_loop(jlo, jd, left_body, 0)
    k_diag, v_diag = load_kv(jd)
    bias_diag = dist_bias(jd)
    accumulate(k_diag, v_diag, lambda c, s: s * c1 - bias_diag - sii[c])
    lax.fori_loop(jd + 1, jhi + 1, right_body, 0)

    bad = jnp.maximum(jnp.max(jnp.where(l_scr[...] <= 2.0 ** ATTN_REDO_LOG2, 0.0, 1.0)),
                      jnp.max(jnp.where(jnp.isfinite(acc_scr[...]), 0.0, 1.0)))

    @pl.when(bad > 0.0)
    def _():
        m_scr[...] = jnp.full(m_scr.shape, NEG_BIG, F32)
        l_scr[...] = jnp.zeros(l_scr.shape, F32)
        acc_scr[...] = jnp.zeros(acc_scr.shape, F32)

        def body(j, carry):
            k, v = load_kv(j)
            bias = dist_bias(j)
            for c in range(2):
                s = _dot_nt(q[:, c * dh:(c + 1) * dh], k[:, c * dh:(c + 1) * dh]) * c1 - bias
                m_old = m_scr[c]
                m_new = jnp.maximum(m_old, jnp.max(s, axis=1, keepdims=True))
                alpha = jnp.exp2(m_old - m_new)
                p = jnp.exp2(s - m_new[:, :1])
                l_scr[c] = alpha * l_scr[c] + jnp.sum(p, axis=1, keepdims=True)
                acc_scr[c] = alpha[:, :1] * acc_scr[c] + _dot(p.astype(BF16), v)
                m_scr[c] = m_new
            return carry

        lax.fori_loop(jlo, jhi + 1, body, 0)

    lq = lq_ref[...]
    lam = (jnp.exp(jnp.sum(lq[0:1] * lq[1:2], axis=1, keepdims=True))
           - jnp.exp(jnp.sum(lq[2:3] * lq[3:4], axis=1, keepdims=True)) + lam_init)
    o = acc_scr[0] / l_scr[0][:, :1] - lam * (acc_scr[1] / l_scr[1][:, :1])
    ms = jnp.mean(o * o, axis=1, keepdims=True)
    o_ref[0] = (o * lax.rsqrt(ms + RMS_EPS) * w_ref[...] * (1.0 - lam_init)).astype(o_ref.dtype)


def _diff_attention(qkv, lambda_qk, subln_w, lam_init):
    bsz, s, _ = qkv.shape
    hw = 2 * ATTN_HEAD_DIM
    tq = _tile(s, 512)
    tk = _tile(s, 1024)
    assert tk % tq == 0
    return pl.pallas_call(
        functools.partial(_attn_kernel, tq=tq, tk=tk, lam_init=lam_init),
        grid=(bsz, ATTN_HEADS, s // tq),
        in_specs=[pl.BlockSpec((4, ATTN_HEAD_DIM), lambda b, h, i: (0, 0)),
                  pl.BlockSpec((1, hw), lambda b, h, i: (0, 0)),
                  pl.BlockSpec((1, tq, hw), lambda b, h, i: (b, i, h)),
                  pl.BlockSpec((1, s, hw), lambda b, h, i: (b, 0, ATTN_HEADS + h)),
                  pl.BlockSpec((1, s, hw), lambda b, h, i: (b, 0, 2 * ATTN_HEADS + h))],
        out_specs=pl.BlockSpec((1, tq, hw), lambda b, h, i: (b, i, h)),
        out_shape=jax.ShapeDtypeStruct((bsz, s, ATTN_HEADS * hw), BF16),
        scratch_shapes=[pltpu.VMEM((2, tq, LANES), F32),
                        pltpu.VMEM((2, tq, LANES), F32),
                        pltpu.VMEM((2, tq, hw), F32),
                        pltpu.VMEM((8, LANES), F32)],
        compiler_params=_cparams("parallel", "parallel", "arbitrary"),
        name="diff_attn",
    )(lambda_qk, subln_w.reshape(1, hw), qkv, qkv, qkv)


def _conv_kernel(prev_ref, cur_ref, next_ref, w_ref, o_ref, *, ts, halo):
    i = pl.program_id(1)
    p = pl.program_id(2)
    ns = pl.num_programs(1)
    dh = DN_HEAD_DIM
    x = cur_ref[0].astype(F32)
    pv = jnp.where(i > 0, prev_ref[0].astype(F32), 0.0)
    nx = jnp.where(i < ns - 1, next_ref[0].astype(F32), 0.0)
    w = w_ref[...]
    row = lax.broadcasted_iota(jnp.int32, (ts, 1), 0)
    xm1 = jnp.where(row == 0, pv[halo - 1:halo], pltpu.roll(x, 1, 0))
    xm2 = jnp.where(row == 0, pv[halo - 2:halo - 1], jnp.where(row == 1, pv[halo - 1:halo], pltpu.roll(x, 2, 0)))
    xp1 = jnp.where(row == ts - 1, nx[0:1], pltpu.roll(x, ts - 1, 0))
    xp2 = jnp.where(row == ts - 2, nx[0:1], jnp.where(row == ts - 1, nx[1:2], pltpu.roll(x, ts - 2, 0)))
    y = w[0:1] * xm2 + w[1:2] * xm1 + w[2:3] * x + w[3:4] * xp1 + w[4:5] * xp2
    y = y * _sigmoid(y)
    qscale = jnp.where(p == 0, dh ** -0.5, 1.0)
    for hh in range(y.shape[1] // dh):
        seg = y[:, hh * dh:(hh + 1) * dh]
        nrm = lax.rsqrt(jnp.sum(seg * seg, axis=1, keepdims=True) + RMS_EPS) * qscale
        fac = jnp.where(p == 2, 1.0, nrm)
        o_ref[0, :, hh * dh:(hh + 1) * dh] = (seg * fac).astype(o_ref.dtype)


def _dn_conv(dn, conv_w):
    bsz, s, _ = dn.shape
    cw = DN_HEADS * DN_HEAD_DIM
    ts = _tile(s, 256)
    halo = 16
    hb = ts // halo
    nhb = s // halo
    return pl.pallas_call(
        functools.partial(_conv_kernel, ts=ts, halo=halo),
        grid=(bsz, s // ts, 3),
        in_specs=[pl.BlockSpec((1, halo, cw), lambda b, i, p: (b, jnp.maximum(i * hb - 1, 0), p)),
                  pl.BlockSpec((1, ts, cw), lambda b, i, p: (b, i, p)),
                  pl.BlockSpec((1, halo, cw), lambda b, i, p: (b, jnp.minimum((i + 1) * hb, nhb - 1), p)),
                  pl.BlockSpec((CONV_K, cw), lambda b, i, p: (0, p))],
        out_specs=pl.BlockSpec((1, ts, cw), lambda b, i, p: (b, i, p)),
        out_shape=jax.ShapeDtypeStruct((bsz, s, 3 * cw), BF16),
        compiler_params=_cparams("parallel", "parallel", "parallel"),
        name="dn_conv",
    )(dn, dn, dn, conv_w)


def _split3(x):
    hi = x.astype(BF16)
    r1 = x - hi.astype(F32)
    mid = r1.astype(BF16)
    lo = (r1 - mid.astype(F32)).astype(BF16)
    return hi, mid, lo


def _softplus(x):
    return jnp.maximum(x, 0.0) + jnp.log1p(jnp.exp(-jnp.abs(x)))


def _gdn_masks(rev):
    r = lax.broadcasted_iota(jnp.int32, (SUPER, SUPER), 0)
    c = lax.broadcasted_iota(jnp.int32, (SUPER, SUPER), 1)
    same = (r // CHUNK) == (c // CHUNK)
    incl = jnp.logical_and(same, (c >= r) if rev else (c <= r))
    strict = jnp.logical_and(same, (c > r) if rev else (c < r))
    return dict(
        incl=incl, strict=strict,
        incl01=jnp.where(incl, 1.0, 0.0).astype(BF16),
        eye=jnp.where(r == c, 1.0, 0.0),
        lvl=31 - lax.clz(lax.bitwise_xor(r, c)))


def _gdn_superchunks(ps):
    dh = DN_HEAD_DIM
    n = range(len(ps))
    mk = [p["mk"] for p in ps]
    kf = [p["k"].astype(F32) for p in ps]
    g_col = [jnp.broadcast_to(p["neg_a"] * _softplus(p["da_col"] + p["dtb"]), (SUPER, dh)) for p in ps]
    beta = [jnp.broadcast_to(_sigmoid(p["db_col"]), (SUPER, dh)) for p in ps]
    gc = [sum(_dot(mk[i]["incl01"], t) for t in _split3(g_col[i])) for i in n]
    tot = [jnp.concatenate([jnp.broadcast_to(gc[i][r:r + 1], (CHUNK, dh))
                            for r in range(0 if ps[i]["rev"] else CHUNK - 1, SUPER, CHUNK)], axis=0) for i in n]
    gc_row = [jnp.transpose(gc[i])[0:1] for i in n]
    eg = [jnp.exp(g) for g in gc]
    ekd = [jnp.exp(tot[i] - gc[i]) for i in n]
    decay = [jnp.exp(jnp.where(mk[i]["incl"], jnp.concatenate([gc[i], gc[i]], axis=1) - gc_row[i], NEG_BIG))
             for i in n]
    kb = [kf[i] * beta[i] for i in n]
    vb = [ps[i]["v"].astype(F32) * beta[i] for i in n]
    mm = [jnp.where(mk[i]["strict"], _dot_nt(kb[i].astype(BF16), ps[i]["k"]) * decay[i], 0.0) for i in n]
    t_inv = [mk[i]["eye"] - jnp.where(mk[i]["lvl"] == 0, mm[i], 0.0) for i in n]
    for j in range(1, 6):
        tb = [t.astype(BF16) for t in t_inv]
        wj = [_dot(tb[i], jnp.where(mk[i]["lvl"] == j, mm[i], 0.0).astype(BF16)) for i in n]
        t_inv = [t_inv[i] - _dot(wj[i].astype(BF16), tb[i]) for i in n]
    rhs = [jnp.concatenate([kb[i] * eg[i], vb[i]], axis=1).astype(BF16) for i in n]
    wub = [_dot(t_inv[i].astype(BF16), rhs[i]).astype(BF16) for i in n]
    qk = [jnp.where(mk[i]["incl"], _dot_nt(ps[i]["q"], ps[i]["k"]) * decay[i], 0.0) for i in n]
    ab = [_dot(qk[i].astype(BF16), wub[i]) for i in n]
    aq = [(ps[i]["q"].astype(F32) * eg[i] - ab[i][:, :dh]).astype(BF16) for i in n]
    kd = [(kf[i] * ekd[i]).astype(BF16) for i in n]
    nch = SUPER // CHUNK
    for step in range(nch):
        for i in n:
            p = ps[i]
            ch = nch - 1 - step if p["rev"] else step
            cr = slice(ch * CHUNK, (ch + 1) * CHUNK)
            st = p["s"][...]
            stb = st.astype(BF16)
            orow = slice(p["row0"] + ch * CHUNK, p["row0"] + (ch + 1) * CHUNK)
            p["o"][0, orow, p["col0"]:p["col0"] + dh] = _dot(aq[i][cr], stb) + ab[i][cr, dh:]
            pq = _dot_tn(kd[i][cr], wub[i][cr])
            egl = jnp.exp(tot[i][ch * CHUNK:ch * CHUNK + 1, :])
            p["s"][...] = egl * st - _dot(pq[:, :dh].astype(BF16), stb) + pq[:, dh:]


def _gdn_kernel(qf_ref, kf_ref, vf_ref, qb_ref, kb_ref, vb_ref, cf_ref, cb_ref, al_ref, dtb_ref,
                of_ref, ob_ref, s_scr, *, ts):
    dh = DN_HEAD_DIM
    hg = pl.program_id(1)
    nsc = ts // SUPER

    @pl.when(pl.program_id(2) == 0)
    def _():
        s_scr[...] = jnp.zeros(s_scr.shape, F32)

    lane = lax.broadcasted_iota(jnp.int32, (1, LANES), 1)
    chains = []
    for d, (q_ref, k_ref, v_ref, c_ref, o_ref) in enumerate(
            ((qf_ref, kf_ref, vf_ref, cf_ref, of_ref), (qb_ref, kb_ref, vb_ref, cb_ref, ob_ref))):
        mk = _gdn_masks(bool(d))
        for hh in range(GDN_HG):
            h = hg * GDN_HG + hh
            chains.append(dict(q=q_ref, k=k_ref, v=v_ref, c=c_ref, o=o_ref, mk=mk, d=d, hh=hh,
                               neg_a=-jnp.exp(al_ref[d, hh]), dtb=dtb_ref[d, hh],
                               asel=lane == d * DN_HEADS + h, bsel=lane == (2 + d) * DN_HEADS + h))

    for step in range(nsc):
        ps = []
        for cc in chains:
            d, hh = cc["d"], cc["hh"]
            sc = nsc - 1 - step if d else step
            rows = slice(sc * SUPER, (sc + 1) * SUPER)
            hc = slice(hh * dh, (hh + 1) * dh)
            cols = cc["c"][0, rows, :]
            ps.append(dict(q=cc["q"][0, rows, hc], k=cc["k"][0, rows, hc], v=cc["v"][0, rows, hc],
                           da_col=jnp.sum(jnp.where(cc["asel"], cols, 0.0), axis=1, keepdims=True),
                           db_col=jnp.sum(jnp.where(cc["bsel"], cols, 0.0), axis=1, keepdims=True),
                           neg_a=cc["neg_a"], dtb=cc["dtb"], mk=cc["mk"],
                           s=s_scr.at[d * GDN_HG + hh], o=cc["o"], row0=sc * SUPER, col0=hh * dh, rev=bool(d)))
        _gdn_superchunks(ps)


def _gdn(qkv, dab, a_log, dt_bias):
    bsz, s, _ = qkv.shape
    dh = DN_HEAD_DIM
    ts = _tile(s, 1024)
    n = s // ts

    def spec(shape, fn):
        return (pl.BlockSpec(shape, lambda b, h, i: fn(b, h, i)),
                pl.BlockSpec(shape, lambda b, h, i: fn(b, h, n - 1 - i)))

    ng = DN_HEADS // GDN_HG
    gw = GDN_HG * dh
    qs = spec((1, ts, gw), lambda b, h, t: (b, t, h))
    ks = spec((1, ts, gw), lambda b, h, t: (b, t, ng + h))
    vs = spec((1, ts, gw), lambda b, h, t: (b, t, 2 * ng + h))
    cs = spec((1, ts, LANES), lambda b, h, t: (b, t, 0))
    par = pl.BlockSpec((2, GDN_HG, 1, 1), lambda b, h, i: (0, h, 0, 0))
    out = jax.ShapeDtypeStruct((bsz, s, DN_HEADS * dh), F32)
    return pl.pallas_call(
        functools.partial(_gdn_kernel, ts=ts),
        grid=(bsz, ng, n),
        in_specs=[qs[0], ks[0], vs[0], qs[1], ks[1], vs[1], cs[0], cs[1], par, par],
        out_specs=list(qs),
        out_shape=[out, out],
        scratch_shapes=[pltpu.VMEM((2 * GDN_HG, dh, dh), F32)],
        compiler_params=_cparams("parallel", "parallel", "arbitrary"),
        name="gdn",
    )(qkv, qkv, qkv, qkv, qkv, qkv, dab, dab,
      a_log.reshape(2, DN_HEADS, 1, 1), dt_bias.reshape(2, DN_HEADS, 1, 1))


def _gdn_out_kernel(of_ref, ob_ref, z_ref, w_ref, o_ref):
    dh = DN_HEAD_DIM
    w = w_ref[...]
    for hh in range(DN_HEADS):
        cs = slice(hh * dh, (hh + 1) * dh)
        o = of_ref[0, :, cs] + ob_ref[0, :, cs]
        z = z_ref[0, :, cs].astype(F32)
        y = o * lax.rsqrt(jnp.mean(o * o, axis=1, keepdims=True) + RMS_EPS) * w
        o_ref[0, :, cs] = (y * (z * _sigmoid(z))).astype(o_ref.dtype)


def _gdn_out(o_f, o_b, dn, norm_w):
    bsz, s, cw = o_f.shape
    ts = _tile(s, 256)
    row = pl.BlockSpec((1, ts, cw), lambda b, i: (b, i, 0))
    return pl.pallas_call(
        _gdn_out_kernel,
        grid=(bsz, s // ts),
        in_specs=[row, row,
                  pl.BlockSpec((1, ts, cw), lambda b, i: (b, i, 3)),
                  pl.BlockSpec((1, DN_HEAD_DIM), lambda b, i: (0, 0))],
        out_specs=row,
        out_shape=jax.ShapeDtypeStruct((bsz, s, cw), BF16),
        compiler_params=_cparams("parallel", "parallel"),
        name="gdn_out",
    )(o_f, o_b, dn, norm_w.reshape(1, DN_HEAD_DIM))


def _merge_kernel(oa_ref, od_ref, wa_ref, wd_ref, ga_ref, gd_ref, o_ref):
    a = _dot(oa_ref[...], wa_ref[...])
    d = _dot(od_ref[...], wd_ref[...])
    o_ref[...] = (_sigmoid(ga_ref[...].astype(F32)) * a + _sigmoid(gd_ref[...].astype(F32)) * d).astype(o_ref.dtype)


def _merge(oa, od, wa, wd, gates):
    m, k = oa.shape
    n = wa.shape[1]
    tm, tn = _tile(m, 1024), _tile(n, 1024)
    nj = n // tn
    return pl.pallas_call(
        _merge_kernel,
        grid=(m // tm, nj),
        in_specs=[pl.BlockSpec((tm, k), lambda i, j: (i, 0)),
                  pl.BlockSpec((tm, k), lambda i, j: (i, 0)),
                  pl.BlockSpec((k, tn), lambda i, j: (0, j)),
                  pl.BlockSpec((k, tn), lambda i, j: (0, j)),
                  pl.BlockSpec((tm, tn), lambda i, j: (i, j)),
                  pl.BlockSpec((tm, tn), lambda i, j: (i, nj + j))],
        out_specs=pl.BlockSpec((tm, tn), lambda i, j: (i, j)),
        out_shape=jax.ShapeDtypeStruct((m, n), BF16),
        compiler_params=_cparams("parallel", "arbitrary"),
        name="branch_merge",
    )(oa, od, wa, wd, gates, gates)


def _topk_rows(x, payload=None):
    n = x.shape[0]
    rid = lax.broadcasted_iota(jnp.int32, x.shape, 0)
    vals, picks = [], []
    for _ in range(PEER_TOPK):
        mx = jnp.max(x, axis=0, keepdims=True)
        first = jnp.min(jnp.where(x == mx, rid, n), axis=0, keepdims=True)
        hit = rid == first
        vals.append(mx)
        picks.append(first if payload is None else jnp.max(jnp.where(hit, payload, -1), axis=0, keepdims=True))
        x = jnp.where(hit, -jnp.inf, x)
    return jnp.concatenate(vals, axis=0), jnp.concatenate(picks, axis=0)


def _route_kernel(q_ref, sk_ref, idx_ref, gate_ref):
    sv, si = [], []
    for p in range(2):
        sc = _dot_nt(sk_ref[0, p].astype(BF16), q_ref[:, p * N_KEYS:(p + 1) * N_KEYS].astype(BF16))
        v, r = _topk_rows(sc)
        sv.append(v)
        si.append(r)
    half = PEER_TOPK // 2

    def pairs(a, b, op):
        return jnp.concatenate([op(a[0:1], b)] + [op(a[k:k + 1], b[0:half]) for k in range(1, half)]
                               + [op(a[half:], b[0:1])], axis=0)

    cand = pairs(sv[0], sv[1], lambda x, y: x + y)
    expert = pairs(si[0], si[1], lambda x, y: x * N_KEYS + y)
    fv, fe = _topk_rows(cand, expert)
    e = jnp.exp(fv - fv[0:1])
    idx_ref[0] = fe
    gate_ref[0] = e / jnp.sum(e, axis=0, keepdims=True)


def _peer_route(q, subkeys):
    t, qw = q.shape
    tt = _tile(t, 512)
    hq = qw // PEER_HEADS
    out = jax.ShapeDtypeStruct((PEER_HEADS, PEER_TOPK, t), jnp.int32)
    idx_t, gate_t = pl.pallas_call(
        _route_kernel,
        grid=(t // tt, PEER_HEADS),
        in_specs=[pl.BlockSpec((tt, hq), lambda i, h: (i, h)),
                  pl.BlockSpec((1, 2, N_KEYS, hq // 2), lambda i, h: (h, 0, 0, 0))],
        out_specs=[pl.BlockSpec((1, PEER_TOPK, tt), lambda i, h: (h, 0, i)),
                   pl.BlockSpec((1, PEER_TOPK, tt), lambda i, h: (h, 0, i))],
        out_shape=[out, jax.ShapeDtypeStruct(out.shape, F32)],
        compiler_params=_cparams("parallel", "parallel"),
        name="peer_route",
    )(q, subkeys)
    ne = PEER_HEADS * PEER_TOPK
    return idx_t.reshape(ne, t).T, gate_t.reshape(ne, t).T


def _gate_kernel(idx_ref, g_ref, o_ref, *, tt):
    ne = idx_ref.shape[1]
    sub = lax.broadcasted_iota(jnp.int32, (N_KEYS, ne), 0)

    def group(gi, carry):
        t0 = pl.multiple_of(gi * GATE_GROUP, GATE_GROUP)
        mats = []
        for u in range(GATE_GROUP):
            idx = idx_ref[pl.ds(t0 + u, 1), :]
            g = g_ref[pl.ds(t0 + u, 1), :]
            i1 = lax.shift_right_logical(idx, 7)
            i2 = lax.bitwise_and(idx, N_KEYS - 1)
            a = jnp.where(sub == i1, g, 0.0).astype(BF16)
            bt = jnp.where(sub == i2, 1.0, 0.0).astype(BF16)
            mats.append(_dot_nt(a, bt))
        planes = pltpu.einshape("tab->atb", jnp.stack(mats, axis=0))
        for a in range(N_KEYS):
            o_ref[pl.ds(t0, GATE_GROUP), a * N_KEYS:(a + 1) * N_KEYS] = planes[a].astype(o_ref.dtype)
        return carry

    lax.fori_loop(0, tt // GATE_GROUP, group, 0)


GATE_GROUP = 16


def _peer_gates(idx, gate):
    t, ne = idx.shape
    tt = _tile(t, 128)
    return pl.pallas_call(
        functools.partial(_gate_kernel, tt=tt),
        grid=(t // tt,),
        in_specs=[pl.BlockSpec((tt, ne), lambda i: (i, 0)),
                  pl.BlockSpec((tt, ne), lambda i: (i, 0))],
        out_specs=pl.BlockSpec((tt, N_KEYS * N_KEYS), lambda i: (i, 0)),
        out_shape=jax.ShapeDtypeStruct((t, N_KEYS * N_KEYS), BF16),
        compiler_params=_cparams("parallel"),
        name="peer_gates",
    )(idx, gate)


def _peer_kernel(x_ref, u_ref, v_ref, g_ref, o_ref):
    n = pl.program_id(1)

    @pl.when(n == 0)
    def _():
        o_ref[...] = jnp.zeros(o_ref.shape, F32)

    hid = _dot_nt(x_ref[...], u_ref[...])
    act = 0.5 * hid * (1.0 + lax.erf(hid * (2.0 ** -0.5)))
    wgt = (g_ref[...].astype(F32) * act).astype(BF16)
    o_ref[...] += _dot(wgt, v_ref[...])


def _peer_dense(x, u, v, g):
    t, d = x.shape
    ne = u.shape[0]
    tm = next(c for c in (384, 256, 128, t) if t % c == 0)
    tn = _tile(ne, 1024)
    return pl.pallas_call(
        _peer_kernel,
        grid=(t // tm, ne // tn),
        in_specs=[pl.BlockSpec((tm, d), lambda i, n: (i, 0)),
                  pl.BlockSpec((tn, d), lambda i, n: (n, 0)),
                  pl.BlockSpec((tn, d), lambda i, n: (n, 0)),
                  pl.BlockSpec((tm, tn), lambda i, n: (i, n))],
        out_specs=pl.BlockSpec((tm, d), lambda i, n: (i, 0)),
        out_shape=jax.ShapeDtypeStruct((t, d), F32),
        compiler_params=_cparams("parallel", "arbitrary"),
        name="peer_dense",
    )(x, u, v, g)


def _token_mixer(h, lam_init, wb, lambda_qk, attn_subln_w, dn_conv_w, dn_a_log, dn_dt_bias, dn_norm_w,
                 wb_br_attn, wb_br_dn, wb_out):
    bsz, s, d = h.shape
    t = bsz * s
    aw = ATTN_HEADS * 2 * ATTN_HEAD_DIM
    dw = DN_HEADS * DN_HEAD_DIM
    h2d = h.reshape(t, d)

    o0, o1, o3 = 3 * aw, 3 * aw + 4 * dw, 3 * aw + 4 * dw + 4 * DN_HEADS
    qkv = _matmul(h2d, wb, BF16, "proj_attn", n=o0).reshape(bsz, s, 3 * aw)
    dn = _matmul(h2d, wb, BF16, "proj_dn", n=o1 - o0, col0=o0).reshape(bsz, s, 4 * dw)
    dab = _matmul(h2d, wb, F32, "proj_ab", n=LANES, col0=o1).reshape(bsz, s, LANES)
    gates = _matmul(h2d, wb[:, o3:], BF16, "proj_gate")

    oa = _diff_attention(qkv, lambda_qk, attn_subln_w, lam_init)

    dqkv = _dn_conv(dn, dn_conv_w)
    o_f, o_b = _gdn(dqkv, dab, dn_a_log, dn_dt_bias)
    od = _gdn_out(o_f, o_b, dn, dn_norm_w)

    merged = _merge(oa.reshape(t, aw), od.reshape(t, dw), wb_br_attn, wb_br_dn, gates)
    return _matmul(merged, wb_out, F32, "out_proj").reshape(bsz, s, d)


def _peer_ffn(h, wb_q, peer_subkeys, ub, vb):
    bsz, s, d = h.shape
    t = bsz * s
    h2d = h.reshape(t, d)
    pq = _matmul(h2d, wb_q, F32, "peer_q")
    idx, gate = _peer_route(pq, peer_subkeys)
    gmat = _peer_gates(idx, gate)
    return _peer_dense(h2d, ub, vb, gmat).reshape(bsz, s, d)


def kernel(x_prompt, x_sample, c_prompt, c_sample, w_ada, b_ada, w_in, lambda_qk, attn_subln_w, dn_conv_w, dn_a_log, dn_dt_bias, dn_norm_w, w_br_attn, w_br_dn, w_out, ln1_g, ln1_b, peer_wq, peer_subkeys, peer_u, peer_v, ln2_g, ln2_b):
    depth = w_ada.shape[0]
    alpha = (2.0 * depth) ** 0.25
    nb_prompt = x_prompt.shape[0]
    x = jnp.concatenate([x_prompt, x_sample], axis=0)
    c = jnp.concatenate([c_prompt, c_sample], axis=0)
    bsz, s, d = x.shape
    c8 = jnp.zeros((8, d), F32).at[:bsz].set(c)

    mods = []
    for l in range(depth):
        mod = _ada(c8, w_ada, b_ada, l)[:bsz]
        mods.append([mod[:, None, j * d:(j + 1) * d] for j in range(6)])

    h = _lnmod(x, mods[0][1], mods[0][0])
    for l in range(depth):
        sh1, sc1, g1, sh2, sc2, g2 = mods[l]
        lam_init = 0.8 - 0.6 * math.exp(-0.3 * l)
        y = _token_mixer(h, lam_init, _to_bf16(w_in, l), lambda_qk[l], attn_subln_w[l], dn_conv_w[l], dn_a_log[l],
                         dn_dt_bias[l], dn_norm_w[l], _to_bf16(w_br_attn, l), _to_bf16(w_br_dn, l),
                         _to_bf16(w_out, l))
        x, h = _res_ln(x, y, g1, ln1_g[l], ln1_b[l], alpha, sc2, sh2)
        y = _peer_ffn(h, _to_bf16(peer_wq, l), peer_subkeys[l], _to_bf16(peer_u, l), _to_bf16(peer_v, l))
        if l + 1 < depth:
            x, h = _res_ln(x, y, g2, ln2_g[l], ln2_b[l], alpha, mods[l + 1][1], mods[l + 1][0])

    l = depth - 1
    g2 = mods[l][5]
    y_prompt, _ = _res_ln(x, y, g2, ln2_g[l], ln2_b[l], alpha, b0=0, nb=nb_prompt)
    y_sample, _ = _res_ln(x, y, g2, ln2_g[l], ln2_b[l], alpha, b0=nb_prompt, nb=bsz - nb_prompt)
    return y_prompt, y_sample
```
